```python
import jax, jax.numpy as jnp
from jax import lax
import numpy as np

D_MODEL = 2048
BATCH = 8
SEQ = 4096
DEPTH = 4

MLA_HEADS = 8
MLA_Q_LORA = 512
MLA_KV_LORA = 512
MLA_NOPE = 128
MLA_ROPE = 64
MLA_V = 128
Q_BLOCK = 128
RET_HEADS = 8
RET_DK = 64
RET_DV = 128
RET_CHUNK = 128
WIN_HEADS = 16
WIN_KV_HEADS = 4
WIN_HEAD_DIM = 64
WINDOW = 128
WIN_BLOCK = 128
N_BRANCH = 3
BRANCH_WIDTH = 1024
N_EXPERTS = 32
TOP_K = 4
D_EXPERT = 768
SWIGLU_LIMIT = 7.0
SWIGLU_ALPHA = 1.702
ROPE_THETA = 10000.0
EPS = 1e-6
NEG_INF = -1e30
ADA_SCALE = 0.5

IN_SIZES = (MLA_Q_LORA, MLA_KV_LORA, MLA_ROPE,
            RET_HEADS * RET_DK, RET_HEADS * RET_DK, RET_HEADS * RET_DV, RET_HEADS * RET_DV,
            WIN_HEADS * WIN_HEAD_DIM, WIN_KV_HEADS * WIN_HEAD_DIM, WIN_KV_HEADS * WIN_HEAD_DIM,
            N_BRANCH * D_MODEL)
IN_WIDTH = int(sum(IN_SIZES))
IN_OFFSETS = tuple(int(v) for v in np.cumsum(IN_SIZES)[:-1])

kernel_name = "hybrid_mla_retention_swa_moe_encoder"


def rms_norm(x, gain):
    xf = x.astype(jnp.float32)
    y = xf * lax.rsqrt(jnp.mean(xf * xf, axis=-1, keepdims=True) + EPS)
    return (y * gain.astype(jnp.float32)).astype(x.dtype)


def modulate(h, shift, scale):
    return h * (1.0 + scale[:, None, :]) + shift[:, None, :]


def rope_tables(seq, dim):
    inv_freq = 1.0 / (ROPE_THETA ** (jnp.arange(0, dim, 2, dtype=jnp.float32) / dim))
    ang = jnp.arange(seq, dtype=jnp.float32)[:, None] * inv_freq[None, :]
    return jnp.cos(ang), jnp.sin(ang)


def apply_rope(x, cos, sin):
    half = x.shape[-1] // 2
    xf = x.astype(jnp.float32)
    x1, x2 = xf[..., :half], xf[..., half:]
    return jnp.concatenate([x1 * cos - x2 * sin, x2 * cos + x1 * sin], axis=-1).astype(x.dtype)


def mla_mixer(c_q, c_kv, k_pe, q_norm, kv_norm, w_uq, w_ukv, cos, sin):
    B, S, _ = c_q.shape
    H = MLA_HEADS
    q = (rms_norm(c_q, q_norm) @ w_uq).reshape(B, S, H, MLA_NOPE + MLA_ROPE)
    q_nope = q[..., :MLA_NOPE]
    q_pe = apply_rope(q[..., MLA_NOPE:], cos[:, None, :], sin[:, None, :])
    kv = (rms_norm(c_kv, kv_norm) @ w_ukv).reshape(B, S, H, MLA_NOPE + MLA_V)
    k_nope, v = kv[..., :MLA_NOPE], kv[..., MLA_NOPE:]
    k_pe = apply_rope(k_pe, cos, sin)
    scale = (MLA_NOPE + MLA_ROPE) ** -0.5
    nb = S // Q_BLOCK
    qn_blocks = q_nope.reshape(B, nb, Q_BLOCK, H, MLA_NOPE).transpose(1, 0, 2, 3, 4)
    qp_blocks = q_pe.reshape(B, nb, Q_BLOCK, H, MLA_ROPE).transpose(1, 0, 2, 3, 4)

    def attend(blk):
        qn, qp = blk
        s = (jnp.einsum("bqhd,bkhd->bhqk", qn, k_nope)
             + jnp.einsum("bqhr,bkr->bhqk", qp, k_pe)).astype(jnp.float32) * scale
        p = jax.nn.softmax(s, axis=-1).astype(v.dtype)
        return jnp.einsum("bhqk,bkhd->bqhd", p, v)

    o = lax.map(attend, (qn_blocks, qp_blocks))
    return o.transpose(1, 0, 2, 3, 4).reshape(B, S, H * MLA_V)


def retention_direction(q, k, v, log_gamma, strict):
    B, S, H, dk = q.shape
    dv = v.shape[-1]
    C = RET_CHUNK
    N = S // C
    qc = q.reshape(B, N, C, H, dk)
    kc = k.reshape(B, N, C, H, dk)
    vc = v.reshape(B, N, C, H, dv)
    idx = jnp.arange(C, dtype=jnp.float32)
    diff = idx[:, None] - idx[None, :]
    mask = (diff > 0) if strict else (diff >= 0)
    decay_intra = jnp.where(mask[None], jnp.exp(log_gamma[:, None, None] * jnp.maximum(diff, 0.0)[None]), 0.0)
    s = jnp.einsum("bnqhd,bnshd->bnhqs", qc, kc) * decay_intra[None, None].astype(q.dtype)
    intra = jnp.einsum("bnhqs,bnshv->bnqhv", s, vc).astype(jnp.float32)
    k_w = jnp.exp(log_gamma[None, :] * (C - 1.0 - idx)[:, None])
    chunk_kv = jnp.einsum("bnshd,sh,bnshv->bnhdv", kc, k_w, vc).astype(jnp.float32)
    chunk_decay = jnp.exp(log_gamma * C)[None, :, None, None]

    def step(state, kv_i):
        return state * chunk_decay + kv_i, state

    init = jnp.zeros((B, H, dk, dv), jnp.float32)
    _, prev = lax.scan(step, init, chunk_kv.transpose(1, 0, 2, 3, 4))
    prev = prev.transpose(1, 0, 2, 3, 4)
    q_w = jnp.exp(log_gamma[None, :] * (idx + 1.0)[:, None])
    cross = jnp.einsum("bnqhd,qh,bnhdv->bnqhv", qc.astype(jnp.float32), q_w, prev)
    return (intra + cross).reshape(B, S, H, dv)


def retention_mixer(r_q, r_k, r_v, r_g, ret_decay, cos, sin):
    B, S, _ = r_q.shape
    H = RET_HEADS
    q = apply_rope(r_q.reshape(B, S, H, RET_DK), cos[:, None, :], sin[:, None, :])
    k = apply_rope(r_k.reshape(B, S, H, RET_DK), cos[:, None, :], sin[:, None, :]) * (RET_DK ** -0.5)
    v = r_v.reshape(B, S, H, RET_DV)
    log_g = -jax.nn.softplus(ret_decay.astype(jnp.float32))
    o_fwd = retention_direction(q, k, v, log_g[0], False)
    o_bwd = retention_direction(q[:, ::-1], k[:, ::-1], v[:, ::-1], log_g[1], True)[:, ::-1]
    o = o_fwd + o_bwd
    mu = jnp.mean(o, axis=-1, keepdims=True)
    var = jnp.mean(jnp.square(o - mu), axis=-1, keepdims=True)
    o = ((o - mu) * lax.rsqrt(var + EPS)).reshape(B, S, H * RET_DV)
    return jax.nn.silu(r_g) * o.astype(r_g.dtype)


def window_mixer(w_q, w_k, w_v, sink, slopes):
    B, S, _ = w_q.shape
    H, G, d = WIN_HEADS, WIN_KV_HEADS, WIN_HEAD_DIM
    R = H // G
    q = w_q.reshape(B, S, G, R, d)
    k = w_k.reshape(B, S, G, d)
    v = w_v.reshape(B, S, G, d)
    pad = WINDOW
    kp = jnp.pad(k, ((0, 0), (pad, pad), (0, 0), (0, 0)))
    vp = jnp.pad(v, ((0, 0), (pad, pad), (0, 0), (0, 0)))
    nb = S // WIN_BLOCK
    span = WIN_BLOCK + 2 * pad
    q_blocks = q.reshape(B, nb, WIN_BLOCK, G, R, d).transpose(1, 0, 2, 3, 4, 5)
    q_off = jnp.arange(WIN_BLOCK)
    k_off = jnp.arange(span)
    dist = jnp.abs(k_off[None, :] - pad - q_off[:, None])
    in_window = dist <= WINDOW
    bias = -slopes.reshape(G, R)[:, :, None, None] * dist[None, None].astype(jnp.float32)
    sink_f = sink.astype(jnp.float32).reshape(G, R)[None, :, :, None, None]
    scale = d ** -0.5

    def attend(args):
        i, qb = args
        start = i * WIN_BLOCK
        kb = lax.dynamic_slice_in_dim(kp, start, span, axis=1)
        vb = lax.dynamic_slice_in_dim(vp, start, span, axis=1)
        key_pos = start - pad + k_off
        valid = in_window & ((key_pos >= 0) & (key_pos < S))[None, :]
        s = jnp.einsum("bqgrd,bkgd->bgrqk", qb, kb).astype(jnp.float32) * scale + bias[None]
        s = jnp.where(valid, s, NEG_INF)
        m = jnp.maximum(jnp.max(s, axis=-1, keepdims=True), sink_f)
        p = jnp.exp(s - m)
        denom = jnp.sum(p, axis=-1, keepdims=True) + jnp.exp(sink_f - m)
        p = (p / denom).astype(vb.dtype)
        return jnp.einsum("bgrqk,bkgd->bqgrd", p, vb)

    o = lax.map(attend, (jnp.arange(nb), q_blocks))
    return o.transpose(1, 0, 2, 3, 4, 5).reshape(B, S, H * d)


def token_mixers(h, w_in, b_gate, q_norm, kv_norm, w_uq, w_ukv, ret_decay, sink,
                 w_br_mla, w_br_ret, w_br_win, w_out, cos, sin, slopes):
    B, S, D = h.shape
    proj = h @ w_in
    (c_q, c_kv, k_pe, r_q, r_k, r_v, r_g, w_q, w_k, w_v, gate_logits) = jnp.split(proj, IN_OFFSETS, axis=-1)
    gates = jax.nn.sigmoid(gate_logits + b_gate).reshape(B, S, N_BRANCH, D)
    y_mla = mla_mixer(c_q, c_kv, k_pe, q_norm, kv_norm, w_uq, w_ukv, cos, sin) @ w_br_mla
    y_ret = retention_mixer(r_q, r_k, r_v, r_g, ret_decay, cos, sin) @ w_br_ret
    y_win = window_mixer(w_q, w_k, w_v, sink, slopes) @ w_br_win
    merged = gates[:, :, 0] * y_mla + gates[:, :, 1] * y_ret + gates[:, :, 2] * y_win
    return merged @ w_out


def moe_ffn(h, w_router, b_router, w_gu, b_gu, w_down, b_down):
    B, S, D = h.shape
    t = h.reshape(B * S, D)
    logits = (t @ w_router).astype(jnp.float32) + b_router.astype(jnp.float32)
    top_val, top_idx = lax.top_k(logits, TOP_K)
    top_w = jax.nn.softmax(top_val, axis=-1)
    combine = jnp.einsum("tk,tke->te", top_w, jax.nn.one_hot(top_idx, N_EXPERTS, dtype=jnp.float32))
    out = jnp.zeros((B * S, D), jnp.float32)
    for e in range(N_EXPERTS):
        gu = t @ w_gu[e] + b_gu[e]
        glu = jnp.minimum(gu[:, :D_EXPERT], SWIGLU_LIMIT)
        lin = jnp.clip(gu[:, D_EXPERT:], -SWIGLU_LIMIT, SWIGLU_LIMIT)
        act = glu * jax.nn.sigmoid(SWIGLU_ALPHA * glu) * (lin + 1.0)
        out = out + combine[:, e:e + 1] * (act @ w_down[e] + b_down[e])
    return out.astype(h.dtype).reshape(B, S, D)


def setup_inputs(seed: int = 0) -> dict:
    key = jax.random.key(seed)
    ks = jax.random.split(key, 25)
    L, D, E, F = DEPTH, D_MODEL, N_EXPERTS, D_EXPERT
    f32 = jnp.float32
    nrm = jax.random.normal
    h_idx = jnp.arange(RET_HEADS, dtype=f32)
    decay_rate = -jnp.log1p(-jnp.exp2(-5.0 - h_idx))
    decay_init = jnp.log(jnp.expm1(decay_rate))
    return {
        "x": nrm(ks[0], (BATCH, SEQ, D), f32),
        "c": nrm(ks[1], (BATCH, D), f32),
        "w_ada": nrm(ks[2], (L, D, 6 * D), f32) * (ADA_SCALE * D ** -0.5),
        "b_ada": 0.02 * nrm(ks[3], (L, 6 * D), f32),
        "norm_mix": 1.0 + 0.02 * nrm(ks[4], (L, D), f32),
        "w_in": nrm(ks[5], (L, D, IN_WIDTH), f32) * D ** -0.5,
        "b_gate": 0.02 * nrm(ks[6], (L, N_BRANCH * D), f32),
        "q_norm": 1.0 + 0.02 * nrm(ks[7], (L, MLA_Q_LORA), f32),
        "kv_norm": 1.0 + 0.02 * nrm(ks[8], (L, MLA_KV_LORA), f32),
        "w_uq": nrm(ks[9], (L, MLA_Q_LORA, MLA_HEADS * (MLA_NOPE + MLA_ROPE)), f32) * MLA_Q_LORA ** -0.5,
        "w_ukv": nrm(ks[10], (L, MLA_KV_LORA, MLA_HEADS * (MLA_NOPE + MLA_V)), f32) * MLA_KV_LORA ** -0.5,
        "ret_decay": decay_init[None, None, :] + 0.05 * nrm(ks[11], (L, 2, RET_HEADS), f32),
        "sink": 0.5 * nrm(ks[12], (L, WIN_HEADS), f32),
        "w_br_mla": nrm(ks[13], (L, MLA_HEADS * MLA_V, D), f32) * (MLA_HEADS * MLA_V) ** -0.5,
        "w_br_ret": nrm(ks[14], (L, RET_HEADS * RET_DV, D), f32) * (RET_HEADS * RET_DV) ** -0.5,
        "w_br_win": nrm(ks[15], (L, WIN_HEADS * WIN_HEAD_DIM, D), f32) * (WIN_HEADS * WIN_HEAD_DIM) ** -0.5,
        "w_out": nrm(ks[16], (L, D, D), f32) * D ** -0.5,
        "norm_ffn": 1.0 + 0.02 * nrm(ks[17], (L, D), f32),
        "w_router": nrm(ks[18], (L, D, E), f32) * D ** -0.5,
        "b_router": 0.01 * nrm(ks[19], (L, E), f32),
        "w_gu": nrm(ks[20], (L, E, D, 2 * F), f32) * D ** -0.5,
        "b_gu": 0.02 * nrm(ks[21], (L, E, 2 * F), f32),
        "w_down": nrm(ks[22], (L, E, F, D), f32) * F ** -0.5,
        "b_down": 0.02 * nrm(ks[23], (L, E, D), f32),
        "norm_final": 1.0 + 0.02 * nrm(ks[24], (D,), f32),
    }


def reference(x, c, w_ada, b_ada, norm_mix, w_in, b_gate, q_norm, kv_norm, w_uq, w_ukv,
              ret_decay, sink, w_br_mla, w_br_ret, w_br_win, w_out, norm_ffn,
              w_router, b_router, w_gu, b_gu, w_down, b_down, norm_final):
    B, S, D = x.shape
    cos, sin = rope_tables(S, MLA_ROPE)
    slopes = jnp.exp2(-8.0 * jnp.arange(1, WIN_HEADS + 1, dtype=jnp.float32) / WIN_HEADS)
    c_act = jax.nn.silu(c)
    for l in range(DEPTH):
        mod = c_act @ w_ada[l] + b_ada[l]
        sh_m, sc_m, g_m, sh_f, sc_f, g_f = jnp.split(mod, 6, axis=-1)
        h = modulate(rms_norm(x, norm_mix[l]), sh_m, sc_m)
        x = x + g_m[:, None, :] * token_mixers(
            h, w_in[l], b_gate[l], q_norm[l], kv_norm[l], w_uq[l], w_ukv[l], ret_decay[l], sink[l],
            w_br_mla[l], w_br_ret[l], w_br_win[l], w_out[l], cos, sin, slopes)
        h = modulate(rms_norm(x, norm_ffn[l]), sh_f, sc_f)
        x = x + g_f[:, None, :] * moe_ffn(h, w_router[l], b_router[l], w_gu[l], b_gu[l], w_down[l], b_down[l])
    return rms_norm(x, norm_final)
```

```python
import functools

import jax
import jax.numpy as jnp
import numpy as np
from jax import lax
from jax.experimental import pallas as pl
from jax.experimental.pallas import tpu as pltpu

F32 = jnp.float32
BF16 = jnp.bfloat16

D_MODEL = 2048
MLA_HEADS = 8
MLA_Q_LORA = 512
MLA_KV_LORA = 512
MLA_NOPE = 128
MLA_ROPE = 64
MLA_V = 128
RET_HEADS = 8
RET_DK = 64
RET_DV = 128
WIN_HEADS = 16
WIN_KV_HEADS = 4
WIN_HEAD_DIM = 64
WINDOW = 128
N_BRANCH = 3
BRANCH_WIDTH = 1024
N_EXPERTS = 32
TOP_K = 4
D_EXPERT = 768
SWIGLU_LIMIT = 7.0
SWIGLU_ALPHA = 1.702
ROPE_THETA = 10000.0
EPS = 1e-6
NEG_INF = -1e30

LANES = 128
VMEM_LIMIT_BYTES = 56 * 1024 * 1024

COL_CQ = 0
COL_CKV = 512
COL_RQ = 1024
COL_RK = 1536
COL_RV = 2048
COL_RG = 3072
COL_WQ = 4096
COL_WK = 5120
COL_WV = 5376
COL_KPE = 5632
COL_KPER = 5760
COL_GATE = 6144
IN_WIDTH_PACKED = COL_GATE + N_BRANCH * D_MODEL


def _params(sem):
    return pltpu.CompilerParams(dimension_semantics=sem, vmem_limit_bytes=VMEM_LIMIT_BYTES)


def _sigmoid(z):
    return 1.0 / (1.0 + jnp.exp(-z))


def _rms(x, gain):
    return x * lax.rsqrt(jnp.mean(x * x, axis=-1, keepdims=True) + EPS) * gain


def _ada_kernel(c_ref, w_ref, b_ref, o_ref):
    c = c_ref[...]
    ca = (c * _sigmoid(c)).astype(BF16)
    o_ref[...] = jnp.dot(ca, w_ref[...].astype(BF16), preferred_element_type=F32) + b_ref[...]


def ada_mod(c, w_ada, b_ada):
    L, D, N = w_ada.shape
    B = c.shape[0]
    tn = 1536
    return pl.pallas_call(
        _ada_kernel,
        grid=(L, N // tn),
        in_specs=[pl.BlockSpec((B, D), lambda l, j: (0, 0)),
                  pl.BlockSpec((None, D, tn), lambda l, j: (l, 0, j)),
                  pl.BlockSpec((None, 1, tn), lambda l, j: (l, 0, j))],
        out_specs=pl.BlockSpec((None, B, tn), lambda l, j: (l, 0, j)),
        out_shape=jax.ShapeDtypeStruct((L, B, N), F32),
        compiler_params=_params(("parallel", "parallel")),
        name="ada_mod",
    )(c, w_ada, b_ada.reshape(L, 1, N))


def _nmm_kernel(x_ref, g_ref, sc_ref, sh_ref, w_ref, o_ref, h_ref):
    @pl.when(pl.program_id(1) == 0)
    def _():
        h_ref[...] = (_rms(x_ref[...], g_ref[...]) * (1.0 + sc_ref[...]) + sh_ref[...]).astype(BF16)

    o_ref[...] = jnp.dot(h_ref[...], w_ref[...], preferred_element_type=F32).astype(o_ref.dtype)


def norm_mod_matmul(x, gain, mod3, k_scale, k_shift, w, S, tm, tn):
    T, D = x.shape
    N = w.shape[1]
    tpb = S // tm
    return pl.pallas_call(
        _nmm_kernel,
        grid=(T // tm, N // tn),
        in_specs=[pl.BlockSpec((tm, D), lambda i, j: (i, 0)),
                  pl.BlockSpec((1, D), lambda i, j: (0, 0)),
                  pl.BlockSpec((None, 1, D), lambda i, j: (i // tpb, 0, k_scale)),
                  pl.BlockSpec((None, 1, D), lambda i, j: (i // tpb, 0, k_shift)),
                  pl.BlockSpec((D, tn), lambda i, j: (0, j))],
        out_specs=pl.BlockSpec((tm, tn), lambda i, j: (i, j)),
        out_shape=jax.ShapeDtypeStruct((T, N), BF16),
        scratch_shapes=[pltpu.VMEM((tm, D), BF16)],
        compiler_params=_params(("parallel", "arbitrary")),
        name="in_proj",
    )(x, gain.reshape(1, D), mod3, mod3, w)


def _mla_prep_kernel(cq_ref, ckv_ref, kpe_ref, kper_ref, qn_ref, kvn_ref, wq_ref, wkv_ref,
                     cos_ref, sin_ref, q_ref, k_ref, v_ref):
    H = MLA_HEADS
    scale = (MLA_NOPE + MLA_ROPE) ** -0.5
    cos = cos_ref[...]
    sin = sin_ref[...]
    yq = _rms(cq_ref[...].astype(F32), qn_ref[...]).astype(BF16)
    for h in range(H):
        z = jnp.dot(yq, wq_ref[:, h * 384:(h + 1) * 384], preferred_element_type=F32)
        q_ref[:, h * 256:h * 256 + 128] = (z[:, :128] * scale).astype(BF16)
        q_ref[:, h * 256 + 128:(h + 1) * 256] = (
            (z[:, 128:256] * cos + z[:, 256:384] * sin) * scale).astype(BF16)
    ykv = _rms(ckv_ref[...].astype(F32), kvn_ref[...]).astype(BF16)
    kpe = (kpe_ref[...].astype(F32) * cos + kper_ref[...].astype(F32) * sin).astype(BF16)
    zk = jnp.dot(ykv, wkv_ref[:, :H * MLA_NOPE], preferred_element_type=F32)
    for h in range(H):
        k_ref[:, h * 256:h * 256 + 128] = zk[:, h * 128:(h + 1) * 128].astype(BF16)
        k_ref[:, h * 256 + 128:(h + 1) * 256] = kpe
    v_ref[...] = jnp.dot(ykv, wkv_ref[:, H * MLA_NOPE:], preferred_element_type=F32).astype(BF16)


def mla_prep(proj, q_norm, kv_norm, wq_p, wkv_p, cos_pad, sin_pad, S, tm):
    T = proj.shape[0]
    H = MLA_HEADS
    tpb = S // tm
    return pl.pallas_call(
        _mla_prep_kernel,
        grid=(T // tm,),
        in_specs=[pl.BlockSpec((tm, 512), lambda i: (i, COL_CQ // 512)),
                  pl.BlockSpec((tm, 512), lambda i: (i, COL_CKV // 512)),
                  pl.BlockSpec((tm, 128), lambda i: (i, COL_KPE // 128)),
                  pl.BlockSpec((tm, 128), lambda i: (i, COL_KPER // 128)),
                  pl.BlockSpec((1, 512), lambda i: (0, 0)),
                  pl.BlockSpec((1, 512), lambda i: (0, 0)),
                  pl.BlockSpec((512, H * 384), lambda i: (0, 0)),
                  pl.BlockSpec((512, H * 256), lambda i: (0, 0)),
                  pl.BlockSpec((tm, 128), lambda i: (i % tpb, 0)),
                  pl.BlockSpec((tm, 128), lambda i: (i % tpb, 0))],
        out_specs=[pl.BlockSpec((tm, H * 256), lambda i: (i, 0)),
                   pl.BlockSpec((tm, H * 256), lambda i: (i, 0)),
                   pl.BlockSpec((tm, H * 128), lambda i: (i, 0))],
        out_shape=[jax.ShapeDtypeStruct((T, H * 256), BF16),
                   jax.ShapeDtypeStruct((T, H * 256), BF16),
                   jax.ShapeDtypeStruct((T, H * 128), BF16)],
        compiler_params=_params(("parallel",)),
        name="mla_prep",
    )(proj, proj, proj, proj, q_norm.reshape(1, 512), kv_norm.reshape(1, 512), wq_p, wkv_p,
      cos_pad, sin_pad)


def _mla_attn_kernel(q_ref, k_ref, v_ref, o_ref):
    s = lax.dot_general(q_ref[...], k_ref[...], (((1,), (1,)), ((), ())),
                        preferred_element_type=F32)
    m = jnp.max(s, axis=-1, keepdims=True)
    p = jnp.exp(s - m)
    l = jnp.sum(p, axis=-1, keepdims=True)
    o = jnp.dot(p.astype(BF16), v_ref[...], preferred_element_type=F32)
    o_ref[...] = (o / l).astype(BF16)


def mla_attention(q, k, v, B, S, tq):
    T = q.shape[0]
    H = MLA_HEADS
    nq = S // tq
    return pl.pallas_call(
        _mla_attn_kernel,
        grid=(B, H, nq),
        in_specs=[pl.BlockSpec((tq, 256), lambda b, h, i: (b * nq + i, h)),
                  pl.BlockSpec((S, 256), lambda b, h, i: (b, h)),
                  pl.BlockSpec((S, 128), lambda b, h, i: (b, h))],
        out_specs=pl.BlockSpec((tq, 128), lambda b, h, i: (b * nq + i, h)),
        out_shape=jax.ShapeDtypeStruct((T, H * MLA_V), BF16),
        compiler_params=_params(("parallel", "parallel", "arbitrary")),
        name="mla_attn",
    )(q, k, v)


def _ret_kernel(lg_ref, q_ref, k_ref, v_ref, g_ref, cos_ref, sin_ref, rmat_ref, ea_ref, eb_ref,
                o_ref, qb_s, kb_s, kv_s, st_s, *, C, N):
    pair = pl.program_id(1)
    rmat = rmat_ref[...]

    def chunk(n):
        return pl.ds(pl.multiple_of(n * C, C), C)

    def rope_body(n, carry):
        sl = chunk(n)
        cos = cos_ref[sl, :]
        sin = sin_ref[sl, :]
        qx = q_ref[sl, :]
        kx = k_ref[sl, :]
        qr = qx.astype(F32) * cos + jnp.dot(qx, rmat, preferred_element_type=F32) * sin
        kr = kx.astype(F32) * cos + jnp.dot(kx, rmat, preferred_element_type=F32) * sin
        qb_s[sl, :] = qr.astype(BF16)
        kb_s[sl, :] = (kr * (RET_DK ** -0.5)).astype(BF16)
        return carry

    lax.fori_loop(0, N, rope_body, 0)

    rowi = lax.broadcasted_iota(jnp.int32, (C, LANES), 0).astype(F32)
    left = lax.broadcasted_iota(jnp.int32, (C, LANES), 1) < RET_DK
    dij = (lax.broadcasted_iota(jnp.int32, (C, C), 0)
           - lax.broadcasted_iota(jnp.int32, (C, C), 1)).astype(F32)
    top = lax.broadcasted_iota(jnp.int32, (2 * RET_DK, RET_DV), 0) < RET_DK

    for hh in range(2):
        lgf = lg_ref[0, pair * 2 + hh]
        lgb = lg_ref[1, pair * 2 + hh]
        emat = (ea_ref if hh == 0 else eb_ref)[...]
        vcol = slice(hh * RET_DV, (hh + 1) * RET_DV)
        q_w = jnp.exp(jnp.where(left, lgf * (rowi + 1.0), lgb * (C - rowi)))
        k_w = jnp.exp(jnp.where(left, lgf * (C - 1.0 - rowi), lgb * rowi))
        d_intra = jnp.exp(jnp.where(dij >= 0, lgf * dij, -lgb * dij))
        dec = jnp.exp(jnp.where(top, lgf * C, lgb * C))

        def kv_body(n, carry):
            sl = chunk(n)
            kd = jnp.dot(kb_s[sl, :], emat, preferred_element_type=F32)
            kdw_t = (kd * k_w).T.astype(BF16)
            kv_s[hh, n] = jnp.dot(kdw_t, v_ref[sl, vcol], preferred_element_type=F32)
            return carry

        lax.fori_loop(0, N, kv_body, 0)

        def fwd_body(n, cf):
            st_s[hh, n, 0:RET_DK, :] = cf[0:RET_DK]
            return cf * dec + kv_s[hh, n]

        lax.fori_loop(0, N, fwd_body, jnp.zeros((2 * RET_DK, RET_DV), F32))

        def bwd_body(t, cb):
            n = N - 1 - t
            st_s[hh, n, RET_DK:2 * RET_DK, :] = cb[RET_DK:2 * RET_DK]
            return cb * dec + kv_s[hh, n]

        lax.fori_loop(0, N, bwd_body, jnp.zeros((2 * RET_DK, RET_DV), F32))

        def out_body(n, carry):
            sl = chunk(n)
            qd = jnp.dot(qb_s[sl, :], emat, preferred_element_type=F32)
            kd = jnp.dot(kb_s[sl, :], emat, preferred_element_type=F32)
            qs = jnp.where(left, qd, 0.0).astype(BF16)
            s = lax.dot_general(qs, kd.astype(BF16), (((1,), (1,)), ((), ())),
                                preferred_element_type=F32)
            sd = (s * d_intra).astype(BF16)
            o = jnp.dot(sd, v_ref[sl, vcol], preferred_element_type=F32)
            o = o + jnp.dot((qd * q_w).astype(BF16), st_s[hh, n].astype(BF16),
                            preferred_element_type=F32)
            mu = jnp.mean(o, axis=-1, keepdims=True)
            dlt = o - mu
            var = jnp.mean(dlt * dlt, axis=-1, keepdims=True)
            on = dlt * lax.rsqrt(var + EPS)
            g = g_ref[sl, vcol].astype(F32)
            o_ref[sl, vcol] = (g * _sigmoid(g) * on).astype(BF16)
            return carry

        lax.fori_loop(0, N, out_body, 0)


def retention(proj, log_g, cos4, sin4, rmat, ea, eb, B, S, C):
    T = proj.shape[0]
    N = S // C
    npair = RET_HEADS // 2
    kern = functools.partial(_ret_kernel, C=C, N=N)
    return pl.pallas_call(
        kern,
        grid=(B, npair),
        in_specs=[pl.BlockSpec(memory_space=pltpu.SMEM),
                  pl.BlockSpec((S, 128), lambda b, p: (b, COL_RQ // 128 + p)),
                  pl.BlockSpec((S, 128), lambda b, p: (b, COL_RK // 128 + p)),
                  pl.BlockSpec((S, 256), lambda b, p: (b, COL_RV // 256 + p)),
                  pl.BlockSpec((S, 256), lambda b, p: (b, COL_RG // 256 + p)),
                  pl.BlockSpec((S, 128), lambda b, p: (0, 0)),
                  pl.BlockSpec((S, 128), lambda b, p: (0, 0)),
                  pl.BlockSpec((128, 128), lambda b, p: (0, 0)),
                  pl.BlockSpec((128, 128), lambda b, p: (0, 0)),
                  pl.BlockSpec((128, 128), lambda b, p: (0, 0))],
        out_specs=pl.BlockSpec((S, 256), lambda b, p: (b, p)),
        out_shape=jax.ShapeDtypeStruct((T, RET_HEADS * RET_DV), BF16),
        scratch_shapes=[pltpu.VMEM((S, 128), BF16),
                        pltpu.VMEM((S, 128), BF16),
                        pltpu.VMEM((2, N, 2 * RET_DK, RET_DV), F32),
                        pltpu.VMEM((2, N, 2 * RET_DK, RET_DV), F32)],
        compiler_params=_params(("parallel", "arbitrary")),
        name="retention",
    )(log_g, proj, proj, proj, proj, cos4, sin4, rmat, ea, eb)


def _win_kernel(sink_ref, q_ref, k_ref, v_ref, o_ref, *, S, tq, span):
    i = pl.program_id(1)
    q0 = i * tq
    kstart = pl.multiple_of(jnp.clip(q0 - WINDOW, 0, S - span), LANES)
    kwin = k_ref[pl.ds(kstart, span), :]
    vwin = v_ref[pl.ds(kstart, span), :]
    qpos = q0 + lax.broadcasted_iota(jnp.int32, (tq, span), 0)
    kpos = kstart + lax.broadcasted_iota(jnp.int32, (tq, span), 1)
    dist_i = jnp.abs(kpos - qpos)
    valid = dist_i <= WINDOW
    dist = dist_i.astype(F32)
    scale = WIN_HEAD_DIM ** -0.5
    R = WIN_HEADS // WIN_KV_HEADS
    d = WIN_HEAD_DIM
    for g in range(WIN_KV_HEADS):
        kg = kwin[:, g * d:(g + 1) * d]
        vg = vwin[:, g * d:(g + 1) * d]
        for r in range(R):
            h = g * R + r
            slope = float(2.0 ** (-8.0 * (h + 1) / WIN_HEADS))
            sink = sink_ref[0, h]
            qh = q_ref[:, h * d:(h + 1) * d]
            s = lax.dot_general(qh, kg, (((1,), (1,)), ((), ())), preferred_element_type=F32)
            s = s * scale - slope * dist
            s = jnp.where(valid, s, NEG_INF)
            m = jnp.maximum(jnp.max(s, axis=-1, keepdims=True), sink)
            p = jnp.exp(s - m)
            denom = jnp.sum(p, axis=-1, keepdims=True) + jnp.exp(sink - m)
            o = jnp.dot((p / denom).astype(BF16), vg, preferred_element_type=F32)
            o_ref[:, h * d:(h + 1) * d] = o.astype(BF16)


def window_attention(proj, sink, B, S, tq):
    T = proj.shape[0]
    nq = S // tq
    span = min(tq + 2 * WINDOW, S)
    kern = functools.partial(_win_kernel, S=S, tq=tq, span=span)
    return pl.pallas_call(
        kern,
        grid=(B, nq),
        in_specs=[pl.BlockSpec(memory_space=pltpu.SMEM),
                  pl.BlockSpec((tq, 1024), lambda b, i: (b * nq + i, COL_WQ // 1024)),
                  pl.BlockSpec((S, 256), lambda b, i: (b, COL_WK // 256)),
                  pl.BlockSpec((S, 256), lambda b, i: (b, COL_WV // 256))],
        out_specs=pl.BlockSpec((tq, 1024), lambda b, i: (b * nq + i, 0)),
        out_shape=jax.ShapeDtypeStruct((T, WIN_HEADS * WIN_HEAD_DIM), BF16),
        compiler_params=_params(("parallel", "arbitrary")),
        name="window_attn",
    )(sink.reshape(1, WIN_HEADS), proj, proj, proj)


def _branch_kernel(om_ref, or_ref, ow_ref, g0_ref, g1_ref, g2_ref, b0_ref, b1_ref, b2_ref,
                   wm_ref, wr_ref, ww_ref, o_ref):
    def term(o, g, b, w):
        gate = _sigmoid(g[...].astype(F32) + b[...])
        return gate * jnp.dot(o[...], w[...], preferred_element_type=F32)

    acc = term(om_ref, g0_ref, b0_ref, wm_ref)
    acc = acc + term(or_ref, g1_ref, b1_ref, wr_ref)
    acc = acc + term(ow_ref, g2_ref, b2_ref, ww_ref)
    o_ref[...] = acc.astype(BF16)


def branch_merge(o_mla, o_ret, o_win, proj, b_gate, w_mla, w_ret, w_win, tm, tn):
    T = proj.shape[0]
    D = D_MODEL
    W = BRANCH_WIDTH
    nj = D // tn
    gate0 = COL_GATE // tn
    o_spec = pl.BlockSpec((tm, W), lambda j, i: (i, 0))
    w_spec = pl.BlockSpec((W, tn), lambda j, i: (0, j))

    def g_spec(br):
        return pl.BlockSpec((tm, tn), lambda j, i: (i, gate0 + br * nj + j))

    def b_spec(br):
        return pl.BlockSpec((1, tn), lambda j, i: (0, br * nj + j))

    return pl.pallas_call(
        _branch_kernel,
        grid=(nj, T // tm),
        in_specs=[o_spec, o_spec, o_spec, g_spec(0), g_spec(1), g_spec(2),
                  b_spec(0), b_spec(1), b_spec(2), w_spec, w_spec, w_spec],
        out_specs=pl.BlockSpec((tm, tn), lambda j, i: (i, j)),
        out_shape=jax.ShapeDtypeStruct((T, D), BF16),
        compiler_params=_params(("parallel", "parallel")),
        name="branch_merge",
    )(o_mla, o_ret, o_win, proj, proj, proj, b_gate, b_gate, b_gate, w_mla, w_ret, w_win)


def _mm_res_kernel(a_ref, w_ref, x_ref, g_ref, o_ref):
    y = jnp.dot(a_ref[...], w_ref[...], preferred_element_type=F32)
    o_ref[...] = x_ref[...] + g_ref[...] * y


def matmul_gated_residual(a, w, x, mod3, k_gate, S, tm, tn):
    T, K = a.shape
    N = w.shape[1]
    tpb = S // tm
    nj = N // tn
    return pl.pallas_call(
        _mm_res_kernel,
        grid=(nj, T // tm),
        in_specs=[pl.BlockSpec((tm, K), lambda j, i: (i, 0)),
                  pl.BlockSpec((K, tn), lambda j, i: (0, j)),
                  pl.BlockSpec((tm, tn), lambda j, i: (i, j)),
                  pl.BlockSpec((None, 1, tn), lambda j, i: (i // tpb, 0, k_gate * nj + j))],
        out_specs=pl.BlockSpec((tm, tn), lambda j, i: (i, j)),
        out_shape=jax.ShapeDtypeStruct((T, N), F32),
        compiler_params=_params(("parallel", "parallel")),
        name="out_proj",
    )(a, w, x, mod3)


def _router_kernel(x_ref, g_ref, sc_ref, sh_ref, wh_ref, wl_ref, b_ref, h_ref, comb_ref):
    h = _rms(x_ref[...], g_ref[...]) * (1.0 + sc_ref[...]) + sh_ref[...]
    h_hi = h.astype(BF16)
    h_ref[...] = h_hi
    h_lo = (h - h_hi.astype(F32)).astype(BF16)
    wh = wh_ref[...]
    logits = (jnp.dot(h_hi, wh, preferred_element_type=F32)
              + jnp.dot(h_lo, wh, preferred_element_type=F32)
              + jnp.dot(h_hi, wl_ref[...], preferred_element_type=F32)) + b_ref[...]
    lane = lax.broadcasted_iota(jnp.int32, logits.shape, 1).astype(F32)
    vals = []
    idxs = []
    l = logits
    for _ in range(TOP_K):
        m = jnp.max(l, axis=-1, keepdims=True)
        idx = jnp.min(jnp.where(l == m, lane, float(LANES)), axis=-1, keepdims=True)
        vals.append(m)
        idxs.append(idx)
        l = jnp.where(lane == idx, -jnp.inf, l)
    es = [jnp.exp(v - vals[0]) for v in vals]
    tot = es[0] + es[1] + es[2] + es[3]
    comb = jnp.zeros(logits.shape, F32)
    for e, idx in zip(es, idxs):
        comb = jnp.where(lane == idx, e / tot, comb)
    comb_ref[...] = comb


def ffn_router(x, gain, mod3, k_scale, k_shift, wr_hi, wr_lo, br_pad, S, tm):
    T, D = x.shape
    tpb = S // tm
    return pl.pallas_call(
        _router_kernel,
        grid=(T // tm,),
        in_specs=[pl.BlockSpec((tm, D), lambda i: (i, 0)),
                  pl.BlockSpec((1, D), lambda i: (0, 0)),
                  pl.BlockSpec((None, 1, D), lambda i: (i // tpb, 0, k_scale)),
                  pl.BlockSpec((None, 1, D), lambda i: (i // tpb, 0, k_shift)),
                  pl.BlockSpec((D, LANES), lambda i: (0, 0)),
                  pl.BlockSpec((D, LANES), lambda i: (0, 0)),
                  pl.BlockSpec((1, LANES), lambda i: (0, 0))],
        out_specs=[pl.BlockSpec((tm, D), lambda i: (i, 0)),
                   pl.BlockSpec((tm, LANES), lambda i: (i, 0))],
        out_shape=[jax.ShapeDtypeStruct((T, D), BF16),
                   jax.ShapeDtypeStruct((T, LANES), F32)],
        compiler_params=_params(("parallel",)),
        name="ffn_router",
    )(x, gain.reshape(1, D), mod3, mod3, wr_hi, wr_lo, br_pad)


def _expert_ffn(h, wgu, bgu, wd, bd):
    F = D_EXPERT
    gu = jnp.dot(h, wgu, preferred_element_type=F32) + bgu
    glu = jnp.minimum(gu[:, :F], SWIGLU_LIMIT)
    lin = jnp.clip(gu[:, F:], -SWIGLU_LIMIT, SWIGLU_LIMIT)
    act = glu * _sigmoid(SWIGLU_ALPHA * glu) * (lin + 1.0)
    return jnp.dot(act.astype(BF16), wd, preferred_element_type=F32) + bd


def _moe_dense_kernel(h_ref, comb_ref, wgu_ref, bgu_ref, wd_ref, bd_ref, x_ref, g_ref, o_ref, acc_ref):
    e = pl.program_id(1)

    @pl.when(e == 0)
    def _():
        acc_ref[...] = jnp.zeros_like(acc_ref)

    y = _expert_ffn(h_ref[...], wgu_ref[...], bgu_ref[...], wd_ref[...], bd_ref[...])
    comb = comb_ref[...]
    lane = lax.broadcasted_iota(jnp.int32, comb.shape, 1)
    col = jnp.sum(jnp.where(lane == e, comb, 0.0), axis=-1, keepdims=True)
    acc_ref[...] += col * y

    @pl.when(e == pl.num_programs(1) - 1)
    def _():
        o_ref[...] = x_ref[...] + g_ref[...] * acc_ref[...]


def moe_dense(h, comb, wgu, bgu, wd, bd, x, mod3, k_gate, S, tm):
    T, D = h.shape
    E, _, F2 = wgu.shape
    F = F2 // 2
    tpb = S // tm
    return pl.pallas_call(
        _moe_dense_kernel,
        grid=(T // tm, E),
        in_specs=[pl.BlockSpec((tm, D), lambda i, e: (i, 0)),
                  pl.BlockSpec((tm, LANES), lambda i, e: (i, 0)),
                  pl.BlockSpec((None, D, F2), lambda i, e: (e, 0, 0)),
                  pl.BlockSpec((None, 1, F2), lambda i, e: (e, 0, 0)),
                  pl.BlockSpec((None, F, D), lambda i, e: (e, 0, 0)),
                  pl.BlockSpec((None, 1, D), lambda i, e: (e, 0, 0)),
                  pl.BlockSpec((tm, D), lambda i, e: (i, 0)),
                  pl.BlockSpec((None, 1, D), lambda i, e: (i // tpb, 0, k_gate))],
        out_specs=pl.BlockSpec((tm, D), lambda i, e: (i, 0)),
        out_shape=jax.ShapeDtypeStruct((T, D), F32),
        scratch_shapes=[pltpu.VMEM((tm, D), F32)],
        compiler_params=_params(("parallel", "arbitrary")),
        name="moe_dense",
    )(h, comb, wgu, bgu.reshape(E, 1, F2), wd, bd.reshape(E, 1, D), x, mod3)


def _final_norm_kernel(x_ref, g_ref, o_ref):
    o_ref[...] = _rms(x_ref[...], g_ref[...])


def final_norm(x, gain, tm):
    T, D = x.shape
    return pl.pallas_call(
        _final_norm_kernel,
        grid=(T // tm,),
        in_specs=[pl.BlockSpec((tm, D), lambda i: (i, 0)),
                  pl.BlockSpec((1, D), lambda i: (0, 0))],
        out_specs=pl.BlockSpec((tm, D), lambda i: (i, 0)),
        out_shape=jax.ShapeDtypeStruct((T, D), F32),
        compiler_params=_params(("parallel",)),
        name="final_norm",
    )(x, gain.reshape(1, D))


def _rot_cols(w):
    half = w.shape[-1] // 2
    return jnp.concatenate([-w[..., half:], w[..., :half]], axis=-1)


def _pad_cols(w, width):
    return jnp.pad(w, ((0, 0), (0, width - w.shape[-1])))


def pack_w_in(w_in):
    D = w_in.shape[0]
    sizes = (MLA_Q_LORA, MLA_KV_LORA, MLA_ROPE, RET_HEADS * RET_DK, RET_HEADS * RET_DK,
             RET_HEADS * RET_DV, RET_HEADS * RET_DV, WIN_HEADS * WIN_HEAD_DIM,
             WIN_KV_HEADS * WIN_HEAD_DIM, WIN_KV_HEADS * WIN_HEAD_DIM, N_BRANCH * D_MODEL)
    offs = np.cumsum((0,) + sizes)
    (c_q, c_kv, k_pe, r_q, r_k, r_v, r_g, w_q, w_k, w_v, gate) = [
        w_in[:, offs[i]:offs[i + 1]] for i in range(len(sizes))]
    filler = jnp.zeros((D, COL_GATE - COL_KPER - LANES), w_in.dtype)
    packed = jnp.concatenate(
        [c_q, c_kv, r_q, r_k, r_v, r_g, w_q, w_k, w_v,
         _pad_cols(k_pe, LANES), _pad_cols(_rot_cols(k_pe), LANES), filler, gate], axis=1)
    return packed.astype(BF16)


def pack_w_uq(w_uq):
    H = MLA_HEADS
    w = w_uq.reshape(MLA_Q_LORA, H, MLA_NOPE + MLA_ROPE)
    nope = w[:, :, :MLA_NOPE]
    pe = w[:, :, MLA_NOPE:]
    z = jnp.zeros((MLA_Q_LORA, H, LANES - MLA_ROPE), w.dtype)
    out = jnp.concatenate([nope, pe, z, _rot_cols(pe), z], axis=-1)
    return out.reshape(MLA_Q_LORA, H * 384).astype(BF16)


def pack_w_ukv(w_ukv):
    H = MLA_HEADS
    w = w_ukv.reshape(MLA_KV_LORA, H, 2, MLA_NOPE)
    return w.transpose(0, 2, 1, 3).reshape(MLA_KV_LORA, 2 * H * MLA_NOPE).astype(BF16)


def rope_consts(S):
    inv_freq = 1.0 / (ROPE_THETA ** (jnp.arange(0, MLA_ROPE, 2, dtype=F32) / MLA_ROPE))
    ang = jnp.arange(S, dtype=F32)[:, None] * inv_freq[None, :]
    cos, sin = jnp.cos(ang), jnp.sin(ang)
    z = jnp.zeros((S, LANES - MLA_ROPE), F32)
    cos_pad = jnp.concatenate([cos, cos, z], axis=1)
    sin_pad = jnp.concatenate([sin, sin, z], axis=1)
    cos4 = jnp.concatenate([cos, cos, cos, cos], axis=1)
    sin4 = jnp.concatenate([sin, sin, sin, sin], axis=1)
    return cos_pad, sin_pad, cos4, sin4


def retention_consts():
    k = np.arange(LANES)[:, None]
    j = np.arange(LANES)[None, :]
    jj = j % RET_DK
    half = RET_DK // 2
    rmat = np.where((jj < half) & (k == j + half), -1.0, 0.0) + np.where((jj >= half) & (k == j - half), 1.0, 0.0)
    ea = np.where((k < RET_DK) & ((j == k) | (j == k + RET_DK)), 1.0, 0.0)
    eb = np.where((k >= RET_DK) & ((j == k) | (j == k - RET_DK)), 1.0, 0.0)
    return (jnp.asarray(rmat, BF16), jnp.asarray(ea, BF16), jnp.asarray(eb, BF16))


def _tile(pref, dim):
    return min(pref, dim)


def kernel(x, c, w_ada, b_ada, norm_mix, w_in, b_gate, q_norm, kv_norm, w_uq, w_ukv, ret_decay, sink,
           w_br_mla, w_br_ret, w_br_win, w_out, norm_ffn, w_router, b_router, w_gu, b_gu, w_down,
           b_down, norm_final):
    B, S, D = x.shape
    L = w_ada.shape[0]
    T = B * S
    E = N_EXPERTS
    cos_pad, sin_pad, cos4, sin4 = rope_consts(S)
    rmat, ea, eb = retention_consts()
    mod = ada_mod(c, w_ada, b_ada)
    xt = x.reshape(T, D)
    tm_big = _tile(1024, S)
    for l in range(L):
        mod3 = mod[l].reshape(B, 1, 6 * D)
        proj = norm_mod_matmul(xt, norm_mix[l], mod3, 1, 0, pack_w_in(w_in[l]), S, tm_big, 1536)
        q, k, v = mla_prep(proj, q_norm[l], kv_norm[l], pack_w_uq(w_uq[l]), pack_w_ukv(w_ukv[l]),
                           cos_pad, sin_pad, S, _tile(512, S))
        o_mla = mla_attention(q, k, v, B, S, _tile(256, S))
        log_g = -jax.nn.softplus(ret_decay[l].astype(F32))
        o_ret = retention(proj, log_g, cos4, sin4, rmat, ea, eb, B, S, _tile(256, S))
        o_win = window_attention(proj, sink[l], B, S, 128)
        merged = branch_merge(o_mla, o_ret, o_win, proj, b_gate[l].reshape(1, N_BRANCH * D),
                              w_br_mla[l].astype(BF16), w_br_ret[l].astype(BF16),
                              w_br_win[l].astype(BF16), _tile(512, S), 1024)
        xt = matmul_gated_residual(merged, w_out[l].astype(BF16), xt, mod3, 2, S, tm_big, 1024)
        wr = jnp.pad(w_router[l], ((0, 0), (0, LANES - E)))
        wr_hi = wr.astype(BF16)
        wr_lo = (wr - wr_hi.astype(F32)).astype(BF16)
        br_pad = jnp.concatenate([b_router[l].astype(F32), jnp.full((LANES - E,), NEG_INF, F32)]).reshape(1, LANES)
        h2, comb = ffn_router(xt, norm_ffn[l], mod3, 4, 3, wr_hi, wr_lo, br_pad, S, _tile(512, S))
        xt = moe_dense(h2, comb, w_gu[l].astype(BF16), b_gu[l], w_down[l].astype(BF16), b_down[l],
                       xt, mod3, 5, S, _tile(512, S))
    out = final_norm(xt, norm_final, tm_big)
    return out.reshape(B, S, D)
```

```python
import functools

import jax
import jax.numpy as jnp
import numpy as np
from jax import lax
from jax.experimental import pallas as pl
from jax.experimental.pallas import tpu as pltpu

F32 = jnp.float32
BF16 = jnp.bfloat16

D_MODEL = 2048
MLA_HEADS = 8
MLA_Q_LORA = 512
MLA_KV_LORA = 512
MLA_NOPE = 128
MLA_ROPE = 64
MLA_V = 128
RET_HEADS = 8
RET_DK = 64
RET_DV = 128
WIN_HEADS = 16
WIN_KV_HEADS = 4
WIN_HEAD_DIM = 64
WINDOW = 128
N_BRANCH = 3
BRANCH_WIDTH = 1024
N_EXPERTS = 32
TOP_K = 4
D_EXPERT = 768
SWIGLU_LIMIT = 7.0
SWIGLU_ALPHA = 1.702
ROPE_THETA = 10000.0
EPS = 1e-6
NEG_INF = -1e30
LOG2_E = 1.4426950408889634

LANES = 128
VMEM_LIMIT_BYTES = 56 * 1024 * 1024

COL_CQ = 0
COL_CKV = 512
COL_RQ = 1024
COL_RK = 1536
COL_RV = 2048
COL_RG = 3072
COL_WQ = 4096
COL_WK = 5120
COL_WV = 5376
COL_KPE = 5632
COL_KPER = 5760
COL_GATE = 6144
IN_WIDTH_PACKED = COL_GATE + N_BRANCH * D_MODEL


def _params(sem):
    return pltpu.CompilerParams(dimension_semantics=sem, vmem_limit_bytes=VMEM_LIMIT_BYTES)


def _sigmoid(z):
    return 1.0 / (1.0 + jnp.exp(-z))


def _rms(x, gain):
    return x * lax.rsqrt(jnp.mean(x * x, axis=-1, keepdims=True) + EPS) * gain


def _ada_kernel(c_ref, w_ref, b_ref, o_ref):
    c = c_ref[...]
    ca = (c * _sigmoid(c)).astype(BF16)
    o_ref[...] = jnp.dot(ca, w_ref[...].astype(BF16), preferred_element_type=F32) + b_ref[...]


def ada_mod(c, w_ada, b_ada):
    L, D, N = w_ada.shape
    B = c.shape[0]
    tn = 1536
    return pl.pallas_call(
        _ada_kernel,
        grid=(L, N // tn),
        in_specs=[pl.BlockSpec((B, D), lambda l, j: (0, 0)),
                  pl.BlockSpec((None, D, tn), lambda l, j: (l, 0, j)),
                  pl.BlockSpec((None, 1, tn), lambda l, j: (l, 0, j))],
        out_specs=pl.BlockSpec((None, B, tn), lambda l, j: (l, 0, j)),
        out_shape=jax.ShapeDtypeStruct((L, B, N), F32),
        compiler_params=_params(("parallel", "parallel")),
        name="ada_mod",
    )(c, w_ada, b_ada.reshape(L, 1, N))


def _nmm_kernel(x_ref, g_ref, sc_ref, sh_ref, w_ref, o_ref, h_ref):
    @pl.when(pl.program_id(1) == 0)
    def _():
        h_ref[...] = (_rms(x_ref[...], g_ref[...]) * (1.0 + sc_ref[...]) + sh_ref[...]).astype(BF16)

    o_ref[...] = jnp.dot(h_ref[...], w_ref[...], preferred_element_type=F32).astype(o_ref.dtype)


def norm_mod_matmul(x, gain, mod3, k_scale, k_shift, w, S, tm, tn):
    T, D = x.shape
    N = w.shape[1]
    tpb = S // tm
    return pl.pallas_call(
        _nmm_kernel,
        grid=(T // tm, N // tn),
        in_specs=[pl.BlockSpec((tm, D), lambda i, j: (i, 0)),
                  pl.BlockSpec((1, D), lambda i, j: (0, 0)),
                  pl.BlockSpec((None, 1, D), lambda i, j: (i // tpb, 0, k_scale)),
                  pl.BlockSpec((None, 1, D), lambda i, j: (i // tpb, 0, k_shift)),
                  pl.BlockSpec((D, tn), lambda i, j: (0, j))],
        out_specs=pl.BlockSpec((tm, tn), lambda i, j: (i, j)),
        out_shape=jax.ShapeDtypeStruct((T, N), BF16),
        scratch_shapes=[pltpu.VMEM((tm, D), BF16)],
        compiler_params=_params(("parallel", "arbitrary")),
        name="in_proj",
    )(x, gain.reshape(1, D), mod3, mod3, w)


def _mla_prep_kernel(cq_ref, ckv_ref, kpe_ref, kper_ref, qn_ref, kvn_ref, wq_ref, wkv_ref,
                     cos_ref, sin_ref, q_ref, k_ref, v_ref):
    H = MLA_HEADS
    scale = (MLA_NOPE + MLA_ROPE) ** -0.5 * LOG2_E
    cos = cos_ref[...]
    sin = sin_ref[...]
    yq = _rms(cq_ref[...].astype(F32), qn_ref[...]).astype(BF16)
    for h in range(H):
        z = jnp.dot(yq, wq_ref[:, h * 384:(h + 1) * 384], preferred_element_type=F32)
        q_ref[:, h * 256:h * 256 + 128] = (z[:, :128] * scale).astype(BF16)
        q_ref[:, h * 256 + 128:(h + 1) * 256] = (
            (z[:, 128:256] * cos + z[:, 256:384] * sin) * scale).astype(BF16)
    ykv = _rms(ckv_ref[...].astype(F32), kvn_ref[...]).astype(BF16)
    kpe = (kpe_ref[...].astype(F32) * cos + kper_ref[...].astype(F32) * sin).astype(BF16)
    zk = jnp.dot(ykv, wkv_ref[:, :H * MLA_NOPE], preferred_element_type=F32)
    for h in range(H):
        k_ref[:, h * 256:h * 256 + 128] = zk[:, h * 128:(h + 1) * 128].astype(BF16)
        k_ref[:, h * 256 + 128:(h + 1) * 256] = kpe
    v_ref[...] = jnp.dot(ykv, wkv_ref[:, H * MLA_NOPE:], preferred_element_type=F32).astype(BF16)


def mla_prep(proj, q_norm, kv_norm, wq_p, wkv_p, cos_pad, sin_pad, S, tm):
    T = proj.shape[0]
    H = MLA_HEADS
    tpb = S // tm
    return pl.pallas_call(
        _mla_prep_kernel,
        grid=(T // tm,),
        in_specs=[pl.BlockSpec((tm, 512), lambda i: (i, COL_CQ // 512)),
                  pl.BlockSpec((tm, 512), lambda i: (i, COL_CKV // 512)),
                  pl.BlockSpec((tm, 128), lambda i: (i, COL_KPE // 128)),
                  pl.BlockSpec((tm, 128), lambda i: (i, COL_KPER // 128)),
                  pl.BlockSpec((1, 512), lambda i: (0, 0)),
                  pl.BlockSpec((1, 512), lambda i: (0, 0)),
                  pl.BlockSpec((512, H * 384), lambda i: (0, 0)),
                  pl.BlockSpec((512, H * 256), lambda i: (0, 0)),
                  pl.BlockSpec((tm, 128), lambda i: (i % tpb, 0)),
                  pl.BlockSpec((tm, 128), lambda i: (i % tpb, 0))],
        out_specs=[pl.BlockSpec((tm, H * 256), lambda i: (i, 0)),
                   pl.BlockSpec((tm, H * 256), lambda i: (i, 0)),
                   pl.BlockSpec((tm, H * 128), lambda i: (i, 0))],
        out_shape=[jax.ShapeDtypeStruct((T, H * 256), BF16),
                   jax.ShapeDtypeStruct((T, H * 256), BF16),
                   jax.ShapeDtypeStruct((T, H * 128), BF16)],
        compiler_params=_params(("parallel",)),
        name="mla_prep",
    )(proj, proj, proj, proj, q_norm.reshape(1, 512), kv_norm.reshape(1, 512), wq_p, wkv_p,
      cos_pad, sin_pad)


def _mla_attn_kernel(q_ref, k_ref, v_ref, o_ref, *, tk):
    q = q_ref[...]
    S = k_ref.shape[0]
    m = l = acc = None
    for c in range(S // tk):
        s = lax.dot_general(q, k_ref[c * tk:(c + 1) * tk, :], (((1,), (1,)), ((), ())),
                            preferred_element_type=F32)
        mc = jnp.max(s, axis=-1, keepdims=True)
        m_new = mc if m is None else jnp.maximum(m, mc)
        p = jnp.exp2(s - m_new)
        pv = jnp.dot(p.astype(BF16), v_ref[c * tk:(c + 1) * tk, :], preferred_element_type=F32)
        ps = jnp.sum(p, axis=-1, keepdims=True)
        if m is None:
            l, acc = ps, pv
        else:
            alpha = jnp.exp2(m - m_new)
            l = alpha * l + ps
            acc = alpha * acc + pv
        m = m_new
    o_ref[...] = (acc / l).astype(BF16)


def mla_attention(q, k, v, B, S, tq, tk):
    T = q.shape[0]
    H = MLA_HEADS
    nq = S // tq
    return pl.pallas_call(
        functools.partial(_mla_attn_kernel, tk=tk),
        grid=(B, H, nq),
        in_specs=[pl.BlockSpec((tq, 256), lambda b, h, i: (b * nq + i, h)),
                  pl.BlockSpec((S, 256), lambda b, h, i: (b, h)),
                  pl.BlockSpec((S, 128), lambda b, h, i: (b, h))],
        out_specs=pl.BlockSpec((tq, 128), lambda b, h, i: (b * nq + i, h)),
        out_shape=jax.ShapeDtypeStruct((T, H * MLA_V), BF16),
        compiler_params=_params(("parallel", "parallel", "arbitrary")),
        name="mla_attn",
    )(q, k, v)


def _ret_kernel(lg_ref, q_ref, k_ref, v_ref, g_ref, cos_ref, sin_ref, rmat_ref, ea_ref, eb_ref,
                o_ref, qd_s, kd_s, kv_s, st_s, *, C, N):
    pair = pl.program_id(1)
    rmat = rmat_ref[...]
    emats = (ea_ref[...], eb_ref[...])
    heads = range(2)

    def chunk(n):
        return pl.ds(pl.multiple_of(n * C, C), C)

    def rope_body(n, carry):
        sl = chunk(n)
        cos = cos_ref[sl, :]
        sin = sin_ref[sl, :]
        qx = q_ref[sl, :]
        kx = k_ref[sl, :]
        qr = qx.astype(F32) * cos + jnp.dot(qx, rmat, preferred_element_type=F32) * sin
        kr = kx.astype(F32) * cos + jnp.dot(kx, rmat, preferred_element_type=F32) * sin
        qb = qr.astype(BF16)
        kb = (kr * (RET_DK ** -0.5)).astype(BF16)
        for hh in heads:
            qd_s[hh, sl, :] = jnp.dot(qb, emats[hh], preferred_element_type=F32).astype(BF16)
            kd_s[hh, sl, :] = jnp.dot(kb, emats[hh], preferred_element_type=F32).astype(BF16)
        return carry

    lax.fori_loop(0, N, rope_body, 0)

    rowi = lax.broadcasted_iota(jnp.int32, (C, LANES), 0).astype(F32)
    left = lax.broadcasted_iota(jnp.int32, (C, LANES), 1) < RET_DK
    dij = (lax.broadcasted_iota(jnp.int32, (C, C), 0)
           - lax.broadcasted_iota(jnp.int32, (C, C), 1)).astype(F32)
    top = lax.broadcasted_iota(jnp.int32, (2 * RET_DK, RET_DV), 0) < RET_DK
    vcols = [slice(hh * RET_DV, (hh + 1) * RET_DV) for hh in heads]
    q_w, k_w, d_intra, dec = [], [], [], []
    for hh in heads:
        lgf = lg_ref[0, pair * 2 + hh]
        lgb = lg_ref[1, pair * 2 + hh]
        q_w.append(jnp.exp(jnp.where(left, lgf * (rowi + 1.0), lgb * (C - rowi))))
        k_w.append(jnp.exp(jnp.where(left, lgf * (C - 1.0 - rowi), lgb * rowi)))
        d_intra.append(jnp.exp(jnp.where(dij >= 0, lgf * dij, -lgb * dij)))
        dec.append(jnp.exp(jnp.where(top, lgf * C, lgb * C)))

    def kv_body(n, carry):
        sl = chunk(n)
        for hh in heads:
            kdw_t = (kd_s[hh, sl, :].astype(F32) * k_w[hh]).T.astype(BF16)
            kv_s[hh, n] = jnp.dot(kdw_t, v_ref[sl, vcols[hh]], preferred_element_type=F32)
        return carry

    lax.fori_loop(0, N, kv_body, 0)

    def fwd_body(n, cf):
        out = []
        for hh in heads:
            st_s[hh, n, 0:RET_DK, :] = cf[hh][0:RET_DK]
            out.append(cf[hh] * dec[hh] + kv_s[hh, n])
        return tuple(out)

    zero = jnp.zeros((2 * RET_DK, RET_DV), F32)
    lax.fori_loop(0, N, fwd_body, (zero, zero))

    def bwd_body(t, cb):
        n = N - 1 - t
        out = []
        for hh in heads:
            st_s[hh, n, RET_DK:2 * RET_DK, :] = cb[hh][RET_DK:2 * RET_DK]
            out.append(cb[hh] * dec[hh] + kv_s[hh, n])
        return tuple(out)

    lax.fori_loop(0, N, bwd_body, (zero, zero))

    def out_body(n, carry):
        sl = chunk(n)
        for hh in heads:
            qd = qd_s[hh, sl, :]
            qs = jnp.where(left, qd, jnp.zeros_like(qd))
            s = lax.dot_general(qs, kd_s[hh, sl, :], (((1,), (1,)), ((), ())),
                                preferred_element_type=F32)
            sd = (s * d_intra[hh]).astype(BF16)
            o = jnp.dot(sd, v_ref[sl, vcols[hh]], preferred_element_type=F32)
            o = o + jnp.dot((qd.astype(F32) * q_w[hh]).astype(BF16), st_s[hh, n].astype(BF16),
                            preferred_element_type=F32)
            mu = jnp.mean(o, axis=-1, keepdims=True)
            dlt = o - mu
            var = jnp.mean(dlt * dlt, axis=-1, keepdims=True)
            on = dlt * lax.rsqrt(var + EPS)
            g = g_ref[sl, vcols[hh]].astype(F32)
            o_ref[sl, vcols[hh]] = (g * _sigmoid(g) * on).astype(BF16)
        return carry

    lax.fori_loop(0, N, out_body, 0)


def retention(proj, log_g, cos4, sin4, rmat, ea, eb, B, S, C):
    T = proj.shape[0]
    N = S // C
    npair = RET_HEADS // 2
    kern = functools.partial(_ret_kernel, C=C, N=N)
    return pl.pallas_call(
        kern,
        grid=(B, npair),
        in_specs=[pl.BlockSpec(memory_space=pltpu.SMEM),
                  pl.BlockSpec((S, 128), lambda b, p: (b, COL_RQ // 128 + p)),
                  pl.BlockSpec((S, 128), lambda b, p: (b, COL_RK // 128 + p)),
                  pl.BlockSpec((S, 256), lambda b, p: (b, COL_RV // 256 + p)),
                  pl.BlockSpec((S, 256), lambda b, p: (b, COL_RG // 256 + p)),
                  pl.BlockSpec((S, 128), lambda b, p: (0, 0)),
                  pl.BlockSpec((S, 128), lambda b, p: (0, 0)),
                  pl.BlockSpec((128, 128), lambda b, p: (0, 0)),
                  pl.BlockSpec((128, 128), lambda b, p: (0, 0)),
                  pl.BlockSpec((128, 128), lambda b, p: (0, 0))],
        out_specs=pl.BlockSpec((S, 256), lambda b, p: (b, p)),
        out_shape=jax.ShapeDtypeStruct((T, RET_HEADS * RET_DV), BF16),
        scratch_shapes=[pltpu.VMEM((2, S, 128), BF16),
                        pltpu.VMEM((2, S, 128), BF16),
                        pltpu.VMEM((2, N, 2 * RET_DK, RET_DV), F32),
                        pltpu.VMEM((2, N, 2 * RET_DK, RET_DV), F32)],
        compiler_params=_params(("parallel", "arbitrary")),
        name="retention",
    )(log_g, proj, proj, proj, proj, cos4, sin4, rmat, ea, eb)


def _win_kernel(sink_ref, q_ref, k_ref, v_ref, o_ref, *, S, tq, span):
    i = pl.program_id(1)
    q0 = i * tq
    kstart = pl.multiple_of(jnp.clip(q0 - WINDOW, 0, S - span), LANES)
    kwin = k_ref[pl.ds(kstart, span), :]
    vwin = v_ref[pl.ds(kstart, span), :]
    qpos = q0 + lax.broadcasted_iota(jnp.int32, (tq, span), 0)
    kpos = kstart + lax.broadcasted_iota(jnp.int32, (tq, span), 1)
    dist_i = jnp.abs(kpos - qpos)
    valid = dist_i <= WINDOW
    dist = dist_i.astype(F32)
    scale = WIN_HEAD_DIM ** -0.5
    R = WIN_HEADS // WIN_KV_HEADS
    d = WIN_HEAD_DIM
    for g in range(WIN_KV_HEADS):
        kg = kwin[:, g * d:(g + 1) * d]
        vg = vwin[:, g * d:(g + 1) * d]
        qg = jnp.concatenate([q_ref[:, (g * R + r) * d:(g * R + r + 1) * d] for r in range(R)], axis=0)
        sg = lax.dot_general(qg, kg, (((1,), (1,)), ((), ())), preferred_element_type=F32)
        ps = []
        inv = []
        for r in range(R):
            h = g * R + r
            slope = float(2.0 ** (-8.0 * (h + 1) / WIN_HEADS))
            sink = sink_ref[0, h]
            s = sg[r * tq:(r + 1) * tq] * scale - slope * dist
            s = jnp.where(valid, s, NEG_INF)
            m = jnp.maximum(jnp.max(s, axis=-1, keepdims=True), sink)
            p = jnp.exp(s - m)
            denom = jnp.sum(p, axis=-1, keepdims=True) + jnp.exp(sink - m)
            ps.append(p.astype(BF16))
            inv.append(1.0 / denom)
        og = jnp.dot(jnp.concatenate(ps, axis=0), vg, preferred_element_type=F32)
        for r in range(R):
            h = g * R + r
            o_ref[:, h * d:(h + 1) * d] = (og[r * tq:(r + 1) * tq] * inv[r]).astype(BF16)


def window_attention(proj, sink, B, S, tq):
    T = proj.shape[0]
    nq = S // tq
    span = min(tq + 2 * WINDOW, S)
    kern = functools.partial(_win_kernel, S=S, tq=tq, span=span)
    return pl.pallas_call(
        kern,
        grid=(B, nq),
        in_specs=[pl.BlockSpec(memory_space=pltpu.SMEM),
                  pl.BlockSpec((tq, 1024), lambda b, i: (b * nq + i, COL_WQ // 1024)),
                  pl.BlockSpec((S, 256), lambda b, i: (b, COL_WK // 256)),
                  pl.BlockSpec((S, 256), lambda b, i: (b, COL_WV // 256))],
        out_specs=pl.BlockSpec((tq, 1024), lambda b, i: (b * nq + i, 0)),
        out_shape=jax.ShapeDtypeStruct((T, WIN_HEADS * WIN_HEAD_DIM), BF16),
        compiler_params=_params(("parallel", "arbitrary")),
        name="window_attn",
    )(sink.reshape(1, WIN_HEADS), proj, proj, proj)


def _branch_kernel(om_ref, or_ref, ow_ref, g0_ref, g1_ref, g2_ref, b0_ref, b1_ref, b2_ref,
                   wm_ref, wr_ref, ww_ref, o_ref):
    def term(o, g, b, w):
        gate = _sigmoid(g[...].astype(F32) + b[...])
        return gate * jnp.dot(o[...], w[...], preferred_element_type=F32)

    acc = term(om_ref, g0_ref, b0_ref, wm_ref)
    acc = acc + term(or_ref, g1_ref, b1_ref, wr_ref)
    acc = acc + term(ow_ref, g2_ref, b2_ref, ww_ref)
    o_ref[...] = acc.astype(BF16)


def branch_merge(o_mla, o_ret, o_win, proj, b_gate, w_mla, w_ret, w_win, tm, tn):
    T = proj.shape[0]
    D = D_MODEL
    W = BRANCH_WIDTH
    nj = D // tn
    gate0 = COL_GATE // tn
    o_spec = pl.BlockSpec((tm, W), lambda j, i: (i, 0))
    w_spec = pl.BlockSpec((W, tn), lambda j, i: (0, j))

    def g_spec(br):
        return pl.BlockSpec((tm, tn), lambda j, i: (i, gate0 + br * nj + j))

    def b_spec(br):
        return pl.BlockSpec((1, tn), lambda j, i: (0, br * nj + j))

    return pl.pallas_call(
        _branch_kernel,
        grid=(nj, T // tm),
        in_specs=[o_spec, o_spec, o_spec, g_spec(0), g_spec(1), g_spec(2),
                  b_spec(0), b_spec(1), b_spec(2), w_spec, w_spec, w_spec],
        out_specs=pl.BlockSpec((tm, tn), lambda j, i: (i, j)),
        out_shape=jax.ShapeDtypeStruct((T, D), BF16),
        compiler_params=_params(("parallel", "parallel")),
        name="branch_merge",
    )(o_mla, o_ret, o_win, proj, proj, proj, b_gate, b_gate, b_gate, w_mla, w_ret, w_win)


def _mm_res_kernel(a_ref, w_ref, x_ref, g_ref, o_ref):
    y = jnp.dot(a_ref[...], w_ref[...], preferred_element_type=F32)
    o_ref[...] = x_ref[...] + g_ref[...] * y


def matmul_gated_residual(a, w, x, mod3, k_gate, S, tm, tn):
    T, K = a.shape
    N = w.shape[1]
    tpb = S // tm
    nj = N // tn
    return pl.pallas_call(
        _mm_res_kernel,
        grid=(nj, T // tm),
        in_specs=[pl.BlockSpec((tm, K), lambda j, i: (i, 0)),
                  pl.BlockSpec((K, tn), lambda j, i: (0, j)),
                  pl.BlockSpec((tm, tn), lambda j, i: (i, j)),
                  pl.BlockSpec((None, 1, tn), lambda j, i: (i // tpb, 0, k_gate * nj + j))],
        out_specs=pl.BlockSpec((tm, tn), lambda j, i: (i, j)),
        out_shape=jax.ShapeDtypeStruct((T, N), F32),
        compiler_params=_params(("parallel", "parallel")),
        name="out_proj",
    )(a, w, x, mod3)


SUBLANES = 8


def _pack_pair(lo, hi):
    lo_bits = pltpu.bitcast(lo.astype(BF16).astype(F32), jnp.uint32)
    hi_bits = pltpu.bitcast(hi.astype(BF16).astype(F32), jnp.uint32)
    return (lo_bits >> 16) | hi_bits


def _unpack_pair(w):
    lo = pltpu.bitcast(w << 16, F32)
    hi = pltpu.bitcast(w & jnp.uint32(0xFFFF0000), F32)
    return lo, hi


def _store_token_tiles(ref, lead, v):
    rows, d = v.shape
    half = d // 2
    for s in range(SUBLANES):
        w = _pack_pair(v[:, s * LANES:(s + 1) * LANES], v[:, half + s * LANES:half + (s + 1) * LANES])
        ref[lead + (pl.ds(s, rows, stride=SUBLANES), slice(None))] = w


def _load_token_tiles(ref, lead, rows, s):
    return _unpack_pair(ref[lead + (pl.ds(s, rows, stride=SUBLANES), slice(None))])


def _router_kernel(x_ref, g_ref, sc_ref, sh_ref, wh_ref, wl_ref, b_ref, hp_ref, ti_ref, tw_ref):
    h = _rms(x_ref[...], g_ref[...]) * (1.0 + sc_ref[...]) + sh_ref[...]
    _store_token_tiles(hp_ref, (), h)
    h_hi = h.astype(BF16)
    h_lo = (h - h_hi.astype(F32)).astype(BF16)
    wh = wh_ref[...]
    logits = (jnp.dot(h_hi, wh, preferred_element_type=F32)
              + jnp.dot(h_lo, wh, preferred_element_type=F32)
              + jnp.dot(h_hi, wl_ref[...], preferred_element_type=F32)) + b_ref[...]
    lane_i = lax.broadcasted_iota(jnp.int32, logits.shape, 1)
    lane = lane_i.astype(F32)
    vals = []
    idxs = []
    l = logits
    for _ in range(TOP_K):
        m = jnp.max(l, axis=-1, keepdims=True)
        idx = jnp.min(jnp.where(l == m, lane, float(LANES)), axis=-1, keepdims=True)
        vals.append(m)
        idxs.append(idx)
        l = jnp.where(lane == idx, -jnp.inf, l)
    es = [jnp.exp(v - vals[0]) for v in vals]
    tot = es[0] + es[1] + es[2] + es[3]
    ti = jnp.zeros(logits.shape, jnp.int32)
    tw = jnp.zeros(logits.shape, F32)
    for k in range(TOP_K):
        ti = jnp.where(lane_i == k, idxs[k].astype(jnp.int32), ti)
        tw = jnp.where(lane_i == k, es[k] / tot, tw)
    ti_ref[...] = ti
    tw_ref[...] = tw


def ffn_router(x, gain, mod3, k_scale, k_shift, wr_hi, wr_lo, br_pad, S, tm):
    T, D = x.shape
    tpb = S // tm
    return pl.pallas_call(
        _router_kernel,
        grid=(T // tm,),
        in_specs=[pl.BlockSpec((tm, D), lambda i: (i, 0)),
                  pl.BlockSpec((1, D), lambda i: (0, 0)),
                  pl.BlockSpec((None, 1, D), lambda i: (i // tpb, 0, k_scale)),
                  pl.BlockSpec((None, 1, D), lambda i: (i // tpb, 0, k_shift)),
                  pl.BlockSpec((D, LANES), lambda i: (0, 0)),
                  pl.BlockSpec((D, LANES), lambda i: (0, 0)),
                  pl.BlockSpec((1, LANES), lambda i: (0, 0))],
        out_specs=[pl.BlockSpec((tm * SUBLANES, LANES), lambda i: (i, 0)),
                   pl.BlockSpec((tm, LANES), lambda i: (i, 0)),
                   pl.BlockSpec((tm, LANES), lambda i: (i, 0))],
        out_shape=[jax.ShapeDtypeStruct((T * SUBLANES, LANES), jnp.uint32),
                   jax.ShapeDtypeStruct((T, LANES), jnp.int32),
                   jax.ShapeDtypeStruct((T, LANES), F32)],
        compiler_params=_params(("parallel",)),
        name="ffn_router",
    )(x, gain.reshape(1, D), mod3, mod3, wr_hi, wr_lo, br_pad)


def moe_plan(topi, tm):
    T = topi.shape[0]
    E = N_EXPERTS
    n_assign = T * TOP_K
    n_tiles = n_assign // tm + E
    n_slots = n_tiles * tm
    flat_e = topi.reshape(-1)
    onehot = (flat_e[:, None] == jnp.arange(E, dtype=jnp.int32)[None, :]).astype(jnp.int32)
    csum = jnp.cumsum(onehot, axis=0)
    rank = jnp.sum(csum * onehot, axis=1) - 1
    counts = csum[-1]
    tiles_e = (counts + tm - 1) // tm
    cum_tiles = jnp.cumsum(tiles_e)
    tile_start = cum_tiles - tiles_e
    slot = jnp.sum(onehot * tile_start[None, :], axis=1) * tm + rank
    a = jnp.arange(n_assign, dtype=jnp.int32)
    t_idx = a // TOP_K
    k_idx = a % TOP_K
    src = jnp.zeros((n_slots,), jnp.int32).at[slot].set(t_idx, unique_indices=True)
    q_idx = jnp.arange(n_slots + tm, dtype=jnp.int32)
    dump = n_assign + ((q_idx // tm) % 2) * tm + q_idx % tm
    dst = dump.at[slot + tm].set(k_idx * T + t_idx, unique_indices=True)
    tile_e = jnp.minimum(
        jnp.searchsorted(cum_tiles, jnp.arange(n_tiles, dtype=jnp.int32), side="right"), E - 1
    ).astype(jnp.int32)
    n_used = cum_tiles[-1:].astype(jnp.int32)
    return (tile_e, n_used, (src * SUBLANES).reshape(n_tiles, 1, tm),
            (dst * SUBLANES).reshape(n_tiles + 1, 1, tm), n_tiles)


def _expert_ffn(h, wgu, bgu, wd, bd):
    F = D_EXPERT
    gu = jnp.dot(h, wgu, preferred_element_type=F32) + bgu
    glu = jnp.minimum(gu[:, :F], SWIGLU_LIMIT)
    lin = jnp.clip(gu[:, F:], -SWIGLU_LIMIT, SWIGLU_LIMIT)
    act = glu * _sigmoid(SWIGLU_ALPHA * glu) * (lin + 1.0)
    return jnp.dot(act.astype(BF16), wd, preferred_element_type=F32) + bd


def _moe_sparse_kernel(te_ref, nu_ref, src_cur, src_nxt, dst_prv, dst_cur, h_hbm, wgu_ref, bgu_ref,
                       wd_ref, bd_ref, y_hbm, hbuf, ybuf, h_s, gsem, ssem, *, tm, dump0):
    i = pl.program_id(0)
    n_used = nu_ref[0]
    slot = i % 2
    other = 1 - slot
    half = h_s.shape[-1] // 2

    def tile_rows(r):
        row0 = r * SUBLANES
        return pl.ds(row0 if isinstance(r, int) else pl.multiple_of(row0, SUBLANES), SUBLANES)

    def gather_start(src_ref, r, s):
        row0 = pl.multiple_of(src_ref[0, r], SUBLANES)
        pltpu.make_async_copy(h_hbm.at[pl.ds(row0, SUBLANES)], hbuf.at[s, tile_rows(r)],
                              gsem.at[s]).start()

    def scatter_start(dst_ref, r, s):
        row0 = pl.multiple_of(dst_ref[0, r], SUBLANES)
        pltpu.make_async_copy(ybuf.at[s, tile_rows(r)], y_hbm.at[pl.ds(row0, SUBLANES)],
                              ssem.at[s]).start()

    def wait_gather(s):
        pltpu.make_async_copy(h_hbm.at[pl.ds(0, tm * SUBLANES)], hbuf.at[s], gsem.at[s]).wait()

    def wait_scatter(s):
        pltpu.make_async_copy(ybuf.at[s], y_hbm.at[pl.ds(0, tm * SUBLANES)], ssem.at[s]).wait()

    @pl.when(i == 0)
    def _():
        def body(r, c):
            gather_start(src_cur, r, 0)
            return c
        lax.fori_loop(0, tm, body, 0, unroll=8)
        ybuf[...] = jnp.zeros(ybuf.shape, ybuf.dtype)
        for s in range(2):
            fill = pltpu.make_async_copy(
                ybuf.at[s], y_hbm.at[pl.ds((dump0 + s * tm) * SUBLANES, tm * SUBLANES)], ssem.at[s])
            fill.start()
            fill.wait()

    @pl.when(i < n_used)
    def _():
        wait_gather(slot)

        @pl.when(i >= 1)
        def _():
            wait_scatter(slot)

        for s in range(SUBLANES):
            lo, hi = _load_token_tiles(hbuf, (slot,), tm, s)
            h_s[:, s * LANES:(s + 1) * LANES] = lo.astype(BF16)
            h_s[:, half + s * LANES:half + (s + 1) * LANES] = hi.astype(BF16)
        for r in range(tm):
            gather_start(src_nxt, r, other)
        for r in range(tm):
            scatter_start(dst_prv, r, other)
        y = _expert_ffn(h_s[...], wgu_ref[...], bgu_ref[...], wd_ref[...], bd_ref[...])
        _store_token_tiles(ybuf, (slot,), y)

        @pl.when(i == n_used - 1)
        def _():
            wait_gather(other)

            def body(r, c):
                scatter_start(dst_cur, r, slot)
                return c
            lax.fori_loop(0, tm, body, 0, unroll=8)
            wait_scatter(other)
            wait_scatter(slot)


def moe_sparse(hp, plan, wgu, bgu, wd, bd, tm):
    T = hp.shape[0] // SUBLANES
    E, D, F2 = wgu.shape
    F = F2 // 2
    assert D == 2 * SUBLANES * LANES
    tile_e, n_used, src, dst, n_tiles = plan
    n_rows = T * TOP_K + 2 * tm
    kern = functools.partial(_moe_sparse_kernel, tm=tm, dump0=T * TOP_K)

    def smem_blk(index_map):
        return pl.BlockSpec((None, 1, tm), index_map, memory_space=pltpu.SMEM)

    grid_spec = pltpu.PrefetchScalarGridSpec(
        num_scalar_prefetch=2,
        grid=(n_tiles,),
        in_specs=[smem_blk(lambda i, te, nu: (i, 0, 0)),
                  smem_blk(lambda i, te, nu: (jnp.minimum(i + 1, n_tiles - 1), 0, 0)),
                  smem_blk(lambda i, te, nu: (i, 0, 0)),
                  smem_blk(lambda i, te, nu: (i + 1, 0, 0)),
                  pl.BlockSpec(memory_space=pltpu.HBM),
                  pl.BlockSpec((None, D, F2), lambda i, te, nu: (te[i], 0, 0)),
                  pl.BlockSpec((None, 1, F2), lambda i, te, nu: (te[i], 0, 0)),
                  pl.BlockSpec((None, F, D), lambda i, te, nu: (te[i], 0, 0)),
                  pl.BlockSpec((None, 1, D), lambda i, te, nu: (te[i], 0, 0))],
        out_specs=pl.BlockSpec(memory_space=pltpu.HBM),
        scratch_shapes=[pltpu.VMEM((2, tm * SUBLANES, LANES), jnp.uint32),
                        pltpu.VMEM((2, tm * SUBLANES, LANES), jnp.uint32),
                        pltpu.VMEM((tm, D), BF16),
                        pltpu.SemaphoreType.DMA((2,)),
                        pltpu.SemaphoreType.DMA((2,))])
    return pl.pallas_call(
        kern,
        grid_spec=grid_spec,
        out_shape=jax.ShapeDtypeStruct((n_rows * SUBLANES, LANES), jnp.uint32),
        compiler_params=_params(("arbitrary",)),
        name="moe_sparse",
    )(tile_e, n_used, src, src, dst, dst, hp, wgu, bgu.reshape(E, 1, F2), wd, bd.reshape(E, 1, D))


def _combine_kernel(y0_ref, y1_ref, y2_ref, y3_ref, tw_ref, x_ref, g_ref, o_ref):
    tw = tw_ref[...]
    tm, d = x_ref.shape
    half = d // 2
    wks = [tw[:, k:k + 1] for k in range(TOP_K)]
    for s in range(SUBLANES):
        acc_lo = None
        acc_hi = None
        for k, y_ref in enumerate((y0_ref, y1_ref, y2_ref, y3_ref)):
            lo, hi = _load_token_tiles(y_ref, (), tm, s)
            acc_lo = wks[k] * lo if acc_lo is None else acc_lo + wks[k] * lo
            acc_hi = wks[k] * hi if acc_hi is None else acc_hi + wks[k] * hi
        c_lo = slice(s * LANES, (s + 1) * LANES)
        c_hi = slice(half + s * LANES, half + (s + 1) * LANES)
        o_ref[:, c_lo] = x_ref[:, c_lo] + g_ref[:, c_lo] * acc_lo
        o_ref[:, c_hi] = x_ref[:, c_hi] + g_ref[:, c_hi] * acc_hi


def moe_combine(y, tw, x, mod3, k_gate, S, tm):
    T, D = x.shape
    tpb = S // tm
    nb = T // tm

    def y_spec(k):
        return pl.BlockSpec((tm * SUBLANES, LANES), lambda i: (k * nb + i, 0))

    return pl.pallas_call(
        _combine_kernel,
        grid=(nb,),
        in_specs=[y_spec(0), y_spec(1), y_spec(2), y_spec(3),
                  pl.BlockSpec((tm, LANES), lambda i: (i, 0)),
                  pl.BlockSpec((tm, D), lambda i: (i, 0)),
                  pl.BlockSpec((None, 1, D), lambda i: (i // tpb, 0, k_gate))],
        out_specs=pl.BlockSpec((tm, D), lambda i: (i, 0)),
        out_shape=jax.ShapeDtypeStruct((T, D), F32),
        compiler_params=_params(("parallel",)),
        name="moe_combine",
    )(y, y, y, y, tw, x, mod3)


def moe_sparse_layer(hp, ti, tw, wgu, bgu, wd, bd, x, mod3, k_gate, S, tm_moe, tm_comb):
    plan = moe_plan(ti[:, :TOP_K], tm_moe)
    y = moe_sparse(hp, plan, wgu, bgu, wd, bd, tm_moe)
    return moe_combine(y, tw, x, mod3, k_gate, S, tm_comb)


def _final_norm_kernel(x_ref, g_ref, o_ref):
    o_ref[...] = _rms(x_ref[...], g_ref[...])


def final_norm(x, gain, tm):
    T, D = x.shape
    return pl.pallas_call(
        _final_norm_kernel,
        grid=(T // tm,),
        in_specs=[pl.BlockSpec((tm, D), lambda i: (i, 0)),
                  pl.BlockSpec((1, D), lambda i: (0, 0))],
        out_specs=pl.BlockSpec((tm, D), lambda i: (i, 0)),
        out_shape=jax.ShapeDtypeStruct((T, D), F32),
        compiler_params=_params(("parallel",)),
        name="final_norm",
    )(x, gain.reshape(1, D))


def _rot_cols(w):
    half = w.shape[-1] // 2
    return jnp.concatenate([-w[..., half:], w[..., :half]], axis=-1)


def _pad_cols(w, width):
    return jnp.pad(w, ((0, 0), (0, width - w.shape[-1])))


def pack_w_in(w_in):
    D = w_in.shape[0]
    sizes = (MLA_Q_LORA, MLA_KV_LORA, MLA_ROPE, RET_HEADS * RET_DK, RET_HEADS * RET_DK,
             RET_HEADS * RET_DV, RET_HEADS * RET_DV, WIN_HEADS * WIN_HEAD_DIM,
             WIN_KV_HEADS * WIN_HEAD_DIM, WIN_KV_HEADS * WIN_HEAD_DIM, N_BRANCH * D_MODEL)
    offs = np.cumsum((0,) + sizes)
    (c_q, c_kv, k_pe, r_q, r_k, r_v, r_g, w_q, w_k, w_v, gate) = [
        w_in[:, offs[i]:offs[i + 1]] for i in range(len(sizes))]
    filler = jnp.zeros((D, COL_GATE - COL_KPER - LANES), w_in.dtype)
    packed = jnp.concatenate(
        [c_q, c_kv, r_q, r_k, r_v, r_g, w_q, w_k, w_v,
         _pad_cols(k_pe, LANES), _pad_cols(_rot_cols(k_pe), LANES), filler, gate], axis=1)
    return packed.astype(BF16)


def pack_w_uq(w_uq):
    H = MLA_HEADS
    w = w_uq.reshape(MLA_Q_LORA, H, MLA_NOPE + MLA_ROPE)
    nope = w[:, :, :MLA_NOPE]
    pe = w[:, :, MLA_NOPE:]
    z = jnp.zeros((MLA_Q_LORA, H, LANES - MLA_ROPE), w.dtype)
    out = jnp.concatenate([nope, pe, z, _rot_cols(pe), z], axis=-1)
    return out.reshape(MLA_Q_LORA, H * 384).astype(BF16)


def pack_w_ukv(w_ukv):
    H = MLA_HEADS
    w = w_ukv.reshape(MLA_KV_LORA, H, 2, MLA_NOPE)
    return w.transpose(0, 2, 1, 3).reshape(MLA_KV_LORA, 2 * H * MLA_NOPE).astype(BF16)


def rope_consts(S):
    inv_freq = 1.0 / (ROPE_THETA ** (jnp.arange(0, MLA_ROPE, 2, dtype=F32) / MLA_ROPE))
    ang = jnp.arange(S, dtype=F32)[:, None] * inv_freq[None, :]
    cos, sin = jnp.cos(ang), jnp.sin(ang)
    z = jnp.zeros((S, LANES - MLA_ROPE), F32)
    cos_pad = jnp.concatenate([cos, cos, z], axis=1)
    sin_pad = jnp.concatenate([sin, sin, z], axis=1)
    cos4 = jnp.concatenate([cos, cos, cos, cos], axis=1)
    sin4 = jnp.concatenate([sin, sin, sin, sin], axis=1)
    return cos_pad, sin_pad, cos4, sin4


def retention_consts():
    k = np.arange(LANES)[:, None]
    j = np.arange(LANES)[None, :]
    jj = j % RET_DK
    half = RET_DK // 2
    rmat = np.where((jj < half) & (k == j + half), -1.0, 0.0) + np.where((jj >= half) & (k == j - half), 1.0, 0.0)
    ea = np.where((k < RET_DK) & ((j == k) | (j == k + RET_DK)), 1.0, 0.0)
    eb = np.where((k >= RET_DK) & ((j == k) | (j == k - RET_DK)), 1.0, 0.0)
    return (jnp.asarray(rmat, BF16), jnp.asarray(ea, BF16), jnp.asarray(eb, BF16))


def _tile(pref, dim):
    return min(pref, dim)


def kernel(x, c, w_ada, b_ada, norm_mix, w_in, b_gate, q_norm, kv_norm, w_uq, w_ukv, ret_decay, sink,
           w_br_mla, w_br_ret, w_br_win, w_out, norm_ffn, w_router, b_router, w_gu, b_gu, w_down,
           b_down, norm_final):
    B, S, D = x.shape
    L = w_ada.shape[0]
    T = B * S
    E = N_EXPERTS
    cos_pad, sin_pad, cos4, sin4 = rope_consts(S)
    rmat, ea, eb = retention_consts()
    mod = ada_mod(c, w_ada, b_ada)
    xt = x.reshape(T, D)
    tm_big = _tile(1024, S)
    for l in range(L):
        mod3 = mod[l].reshape(B, 1, 6 * D)
        proj = norm_mod_matmul(xt, norm_mix[l], mod3, 1, 0, pack_w_in(w_in[l]), S, tm_big, 1536)
        q, k, v = mla_prep(proj, q_norm[l], kv_norm[l], pack_w_uq(w_uq[l]), pack_w_ukv(w_ukv[l]),
                           cos_pad, sin_pad, S, _tile(512, S))
        o_mla = mla_attention(q, k, v, B, S, _tile(512, S), _tile(512, S))
        log_g = -jax.nn.softplus(ret_decay[l].astype(F32))
        o_ret = retention(proj, log_g, cos4, sin4, rmat, ea, eb, B, S, _tile(256, S))
        o_win = window_attention(proj, sink[l], B, S, 128)
        merged = branch_merge(o_mla, o_ret, o_win, proj, b_gate[l].reshape(1, N_BRANCH * D),
                              w_br_mla[l].astype(BF16), w_br_ret[l].astype(BF16),
                              w_br_win[l].astype(BF16), _tile(512, S), 1024)
        xt = matmul_gated_residual(merged, w_out[l].astype(BF16), xt, mod3, 2, S, tm_big, 1024)
        wr = jnp.pad(w_router[l], ((0, 0), (0, LANES - E)))
        wr_hi = wr.astype(BF16)
        wr_lo = (wr - wr_hi.astype(F32)).astype(BF16)
        br_pad = jnp.concatenate([b_router[l].astype(F32), jnp.full((LANES - E,), NEG_INF, F32)]).reshape(1, LANES)
        hp, ti, tw = ffn_router(xt, norm_ffn[l], mod3, 4, 3, wr_hi, wr_lo, br_pad, S, _tile(512, S))
        xt = moe_sparse_layer(hp, ti, tw, w_gu[l].astype(BF16), b_gu[l], w_down[l].astype(BF16),
                              b_down[l], xt, mod3, 5, S, _tile(512, S), _tile(512, S))
    out = final_norm(xt, norm_final, tm_big)
    return out.reshape(B, S, D)
```

```python
import functools

import jax
import jax.numpy as jnp
import numpy as np
from jax import lax
from jax.experimental import pallas as pl
from jax.experimental.pallas import tpu as pltpu

F32 = jnp.float32
BF16 = jnp.bfloat16

D_MODEL = 2048
MLA_HEADS = 8
MLA_Q_LORA = 512
MLA_KV_LORA = 512
MLA_NOPE = 128
MLA_ROPE = 64
MLA_V = 128
RET_HEADS = 8
RET_DK = 64
RET_DV = 128
WIN_HEADS = 16
WIN_KV_HEADS = 4
WIN_HEAD_DIM = 64
WINDOW = 128
N_BRANCH = 3
BRANCH_WIDTH = 1024
N_EXPERTS = 32
TOP_K = 4
D_EXPERT = 768
SWIGLU_LIMIT = 7.0
SWIGLU_ALPHA = 1.702
ROPE_THETA = 10000.0
EPS = 1e-6
NEG_INF = -1e30
LOG2_E = 1.4426950408889634

LANES = 128
VMEM_LIMIT_BYTES = 56 * 1024 * 1024

COL_CQ = 0
COL_CKV = 512
COL_RQ = 1024
COL_RK = 1536
COL_RV = 2048
COL_RG = 3072
COL_WQ = 4096
COL_WK = 5120
COL_WV = 5376
COL_KPE = 5632
COL_KPER = 5760
COL_GATE = 6144
IN_WIDTH_PACKED = COL_GATE + N_BRANCH * D_MODEL


def _params(sem):
    return pltpu.CompilerParams(dimension_semantics=sem, vmem_limit_bytes=VMEM_LIMIT_BYTES)


def _sigmoid(z):
    return 1.0 / (1.0 + jnp.exp(-z))


def _rms(x, gain):
    return x * lax.rsqrt(jnp.mean(x * x, axis=-1, keepdims=True) + EPS) * gain


def _ada_kernel(c_ref, w_ref, b_ref, o_ref):
    c = c_ref[...]
    ca = (c * _sigmoid(c)).astype(BF16)
    o_ref[...] = jnp.dot(ca, w_ref[...].astype(BF16), preferred_element_type=F32) + b_ref[...]


def ada_mod(c, w_ada, b_ada):
    L, D, N = w_ada.shape
    B = c.shape[0]
    tn = 1536
    return pl.pallas_call(
        _ada_kernel,
        grid=(L, N // tn),
        in_specs=[pl.BlockSpec((B, D), lambda l, j: (0, 0)),
                  pl.BlockSpec((None, D, tn), lambda l, j: (l, 0, j)),
                  pl.BlockSpec((None, 1, tn), lambda l, j: (l, 0, j))],
        out_specs=pl.BlockSpec((None, B, tn), lambda l, j: (l, 0, j)),
        out_shape=jax.ShapeDtypeStruct((L, B, N), F32),
        compiler_params=_params(("parallel", "parallel")),
        name="ada_mod",
    )(c, w_ada, b_ada.reshape(L, 1, N))


def _nmm_kernel(x_ref, g_ref, sc_ref, sh_ref, w_ref, o_ref, h_ref):
    @pl.when(pl.program_id(1) == 0)
    def _():
        h_ref[...] = (_rms(x_ref[...], g_ref[...]) * (1.0 + sc_ref[...]) + sh_ref[...]).astype(BF16)

    o_ref[...] = jnp.dot(h_ref[...], w_ref[...], preferred_element_type=F32).astype(o_ref.dtype)


def norm_mod_matmul(x, gain, mod3, k_scale, k_shift, w, S, tm, tn):
    T, D = x.shape
    N = w.shape[1]
    tpb = S // tm
    return pl.pallas_call(
        _nmm_kernel,
        grid=(T // tm, N // tn),
        in_specs=[pl.BlockSpec((tm, D), lambda i, j: (i, 0)),
                  pl.BlockSpec((1, D), lambda i, j: (0, 0)),
                  pl.BlockSpec((None, 1, D), lambda i, j: (i // tpb, 0, k_scale)),
                  pl.BlockSpec((None, 1, D), lambda i, j: (i // tpb, 0, k_shift)),
                  pl.BlockSpec((D, tn), lambda i, j: (0, j))],
        out_specs=pl.BlockSpec((tm, tn), lambda i, j: (i, j)),
        out_shape=jax.ShapeDtypeStruct((T, N), BF16),
        scratch_shapes=[pltpu.VMEM((tm, D), BF16)],
        compiler_params=_params(("parallel", "arbitrary")),
        name="in_proj",
    )(x, gain.reshape(1, D), mod3, mod3, w)


def _mla_prep_kernel(cq_ref, ckv_ref, kpe_ref, kper_ref, qn_ref, kvn_ref, wq_ref, wkv_ref,
                     cos_ref, sin_ref, q_ref, k_ref, v_ref):
    H = MLA_HEADS
    scale = (MLA_NOPE + MLA_ROPE) ** -0.5 * LOG2_E
    cos = cos_ref[...]
    sin = sin_ref[...]
    yq = _rms(cq_ref[...].astype(F32), qn_ref[...]).astype(BF16)
    for h in range(H):
        z = jnp.dot(yq, wq_ref[:, h * 384:(h + 1) * 384], preferred_element_type=F32)
        q_ref[:, h * 256:h * 256 + 128] = (z[:, :128] * scale).astype(BF16)
        q_ref[:, h * 256 + 128:(h + 1) * 256] = (
            (z[:, 128:256] * cos + z[:, 256:384] * sin) * scale).astype(BF16)
    ykv = _rms(ckv_ref[...].astype(F32), kvn_ref[...]).astype(BF16)
    kpe = (kpe_ref[...].astype(F32) * cos + kper_ref[...].astype(F32) * sin).astype(BF16)
    zk = jnp.dot(ykv, wkv_ref[:, :H * MLA_NOPE], preferred_element_type=F32)
    for h in range(H):
        k_ref[:, h * 256:h * 256 + 128] = zk[:, h * 128:(h + 1) * 128].astype(BF16)
        k_ref[:, h * 256 + 128:(h + 1) * 256] = kpe
    v_ref[...] = jnp.dot(ykv, wkv_ref[:, H * MLA_NOPE:], preferred_element_type=F32).astype(BF16)


def mla_prep(proj, q_norm, kv_norm, wq_p, wkv_p, cos_pad, sin_pad, S, tm):
    T = proj.shape[0]
    H = MLA_HEADS
    tpb = S // tm
    return pl.pallas_call(
        _mla_prep_kernel,
        grid=(T // tm,),
        in_specs=[pl.BlockSpec((tm, 512), lambda i: (i, COL_CQ // 512)),
                  pl.BlockSpec((tm, 512), lambda i: (i, COL_CKV // 512)),
                  pl.BlockSpec((tm, 128), lambda i: (i, COL_KPE // 128)),
                  pl.BlockSpec((tm, 128), lambda i: (i, COL_KPER // 128)),
                  pl.BlockSpec((1, 512), lambda i: (0, 0)),
                  pl.BlockSpec((1, 512), lambda i: (0, 0)),
                  pl.BlockSpec((512, H * 384), lambda i: (0, 0)),
                  pl.BlockSpec((512, H * 256), lambda i: (0, 0)),
                  pl.BlockSpec((tm, 128), lambda i: (i % tpb, 0)),
                  pl.BlockSpec((tm, 128), lambda i: (i % tpb, 0))],
        out_specs=[pl.BlockSpec((tm, H * 256), lambda i: (i, 0)),
                   pl.BlockSpec((tm, H * 256), lambda i: (i, 0)),
                   pl.BlockSpec((tm, H * 128), lambda i: (i, 0))],
        out_shape=[jax.ShapeDtypeStruct((T, H * 256), BF16),
                   jax.ShapeDtypeStruct((T, H * 256), BF16),
                   jax.ShapeDtypeStruct((T, H * 128), BF16)],
        compiler_params=_params(("parallel",)),
        name="mla_prep",
    )(proj, proj, proj, proj, q_norm.reshape(1, 512), kv_norm.reshape(1, 512), wq_p, wkv_p,
      cos_pad, sin_pad)


def _mla_attn_kernel(q_ref, k_ref, v_ref, o_ref, *, tk):
    q = q_ref[...]
    S = k_ref.shape[0]
    m = l = acc = None
    for c in range(S // tk):
        s = lax.dot_general(q, k_ref[c * tk:(c + 1) * tk, :], (((1,), (1,)), ((), ())),
                            preferred_element_type=F32)
        mc = jnp.max(s, axis=-1, keepdims=True)
        m_new = mc if m is None else jnp.maximum(m, mc)
        p = jnp.exp2(s - m_new)
        pv = jnp.dot(p.astype(BF16), v_ref[c * tk:(c + 1) * tk, :], preferred_element_type=F32)
        ps = jnp.sum(p, axis=-1, keepdims=True)
        if m is None:
            l, acc = ps, pv
        else:
            alpha = jnp.exp2(m - m_new)
            l = alpha * l + ps
            acc = alpha * acc + pv
        m = m_new
    o_ref[...] = (acc / l).astype(BF16)


def mla_attention(q, k, v, B, S, tq, tk):
    T = q.shape[0]
    H = MLA_HEADS
    nq = S // tq
    return pl.pallas_call(
        functools.partial(_mla_attn_kernel, tk=tk),
        grid=(B, H, nq),
        in_specs=[pl.BlockSpec((tq, 256), lambda b, h, i: (b * nq + i, h)),
                  pl.BlockSpec((S, 256), lambda b, h, i: (b, h)),
                  pl.BlockSpec((S, 128), lambda b, h, i: (b, h))],
        out_specs=pl.BlockSpec((tq, 128), lambda b, h, i: (b * nq + i, h)),
        out_shape=jax.ShapeDtypeStruct((T, H * MLA_V), BF16),
        compiler_params=_params(("parallel", "parallel", "arbitrary")),
        name="mla_attn",
    )(q, k, v)


def _ret_kernel(lg_ref, q_ref, k_ref, v_ref, g_ref, cos_ref, sin_ref, rmat_ref, ea_ref, eb_ref,
                o_ref, qd_s, kd_s, kv_s, st_s, *, C, N):
    pair = pl.program_id(1)
    rmat = rmat_ref[...]
    emats = (ea_ref[...], eb_ref[...])
    heads = range(2)

    def chunk(n):
        return pl.ds(pl.multiple_of(n * C, C), C)

    def rope_body(n, carry):
        sl = chunk(n)
        cos = cos_ref[sl, :]
        sin = sin_ref[sl, :]
        qx = q_ref[sl, :]
        kx = k_ref[sl, :]
        qr = qx.astype(F32) * cos + jnp.dot(qx, rmat, preferred_element_type=F32) * sin
        kr = kx.astype(F32) * cos + jnp.dot(kx, rmat, preferred_element_type=F32) * sin
        qb = qr.astype(BF16)
        kb = (kr * (RET_DK ** -0.5)).astype(BF16)
        for hh in heads:
            qd_s[hh, sl, :] = jnp.dot(qb, emats[hh], preferred_element_type=F32).astype(BF16)
            kd_s[hh, sl, :] = jnp.dot(kb, emats[hh], preferred_element_type=F32).astype(BF16)
        return carry

    lax.fori_loop(0, N, rope_body, 0)

    rowi = lax.broadcasted_iota(jnp.int32, (C, LANES), 0).astype(F32)
    left = lax.broadcasted_iota(jnp.int32, (C, LANES), 1) < RET_DK
    dij = (lax.broadcasted_iota(jnp.int32, (C, C), 0)
           - lax.broadcasted_iota(jnp.int32, (C, C), 1)).astype(F32)
    top = lax.broadcasted_iota(jnp.int32, (2 * RET_DK, RET_DV), 0) < RET_DK
    vcols = [slice(hh * RET_DV, (hh + 1) * RET_DV) for hh in heads]
    q_w, k_w, d_intra, dec = [], [], [], []
    for hh in heads:
        lgf = lg_ref[0, pair * 2 + hh]
        lgb = lg_ref[1, pair * 2 + hh]
        q_w.append(jnp.exp(jnp.where(left, lgf * (rowi + 1.0), lgb * (C - rowi))))
        k_w.append(jnp.exp(jnp.where(left, lgf * (C - 1.0 - rowi), lgb * rowi)))
        d_intra.append(jnp.exp(jnp.where(dij >= 0, lgf * dij, -lgb * dij)))
        dec.append(jnp.exp(jnp.where(top, lgf * C, lgb * C)))

    def kv_body(n, carry):
        sl = chunk(n)
        for hh in heads:
            kdw_t = (kd_s[hh, sl, :].astype(F32) * k_w[hh]).T.astype(BF16)
            kv_s[hh, n] = jnp.dot(kdw_t, v_ref[sl, vcols[hh]], preferred_element_type=F32)
        return carry

    lax.fori_loop(0, N, kv_body, 0)

    def fwd_body(n, cf):
        out = []
        for hh in heads:
            st_s[hh, n, 0:RET_DK, :] = cf[hh][0:RET_DK]
            out.append(cf[hh] * dec[hh] + kv_s[hh, n])
        return tuple(out)

    zero = jnp.zeros((2 * RET_DK, RET_DV), F32)
    lax.fori_loop(0, N, fwd_body, (zero, zero))

    def bwd_body(t, cb):
        n = N - 1 - t
        out = []
        for hh in heads:
            st_s[hh, n, RET_DK:2 * RET_DK, :] = cb[hh][RET_DK:2 * RET_DK]
            out.append(cb[hh] * dec[hh] + kv_s[hh, n])
        return tuple(out)

    lax.fori_loop(0, N, bwd_body, (zero, zero))

    def out_body(n, carry):
        sl = chunk(n)
        for hh in heads:
            qd = qd_s[hh, sl, :]
            qs = jnp.where(left, qd, jnp.zeros_like(qd))
            s = lax.dot_general(qs, kd_s[hh, sl, :], (((1,), (1,)), ((), ())),
                                preferred_element_type=F32)
            sd = (s * d_intra[hh]).astype(BF16)
            o = jnp.dot(sd, v_ref[sl, vcols[hh]], preferred_element_type=F32)
            o = o + jnp.dot((qd.astype(F32) * q_w[hh]).astype(BF16), st_s[hh, n].astype(BF16),
                            preferred_element_type=F32)
            mu = jnp.mean(o, axis=-1, keepdims=True)
            dlt = o - mu
            var = jnp.mean(dlt * dlt, axis=-1, keepdims=True)
            on = dlt * lax.rsqrt(var + EPS)
            g = g_ref[sl, vcols[hh]].astype(F32)
            o_ref[sl, vcols[hh]] = (g * _sigmoid(g) * on).astype(BF16)
        return carry

    lax.fori_loop(0, N, out_body, 0)


def retention(proj, log_g, cos4, sin4, rmat, ea, eb, B, S, C):
    T = proj.shape[0]
    N = S // C
    npair = RET_HEADS // 2
    kern = functools.partial(_ret_kernel, C=C, N=N)
    return pl.pallas_call(
        kern,
        grid=(B, npair),
        in_specs=[pl.BlockSpec(memory_space=pltpu.SMEM),
                  pl.BlockSpec((S, 128), lambda b, p: (b, COL_RQ // 128 + p)),
                  pl.BlockSpec((S, 128), lambda b, p: (b, COL_RK // 128 + p)),
                  pl.BlockSpec((S, 256), lambda b, p: (b, COL_RV // 256 + p)),
                  pl.BlockSpec((S, 256), lambda b, p: (b, COL_RG // 256 + p)),
                  pl.BlockSpec((S, 128), lambda b, p: (0, 0)),
                  pl.BlockSpec((S, 128), lambda b, p: (0, 0)),
                  pl.BlockSpec((128, 128), lambda b, p: (0, 0)),
                  pl.BlockSpec((128, 128), lambda b, p: (0, 0)),
                  pl.BlockSpec((128, 128), lambda b, p: (0, 0))],
        out_specs=pl.BlockSpec((S, 256), lambda b, p: (b, p)),
        out_shape=jax.ShapeDtypeStruct((T, RET_HEADS * RET_DV), BF16),
        scratch_shapes=[pltpu.VMEM((2, S, 128), BF16),
                        pltpu.VMEM((2, S, 128), BF16),
                        pltpu.VMEM((2, N, 2 * RET_DK, RET_DV), F32),
                        pltpu.VMEM((2, N, 2 * RET_DK, RET_DV), F32)],
        compiler_params=_params(("parallel", "arbitrary")),
        name="retention",
    )(log_g, proj, proj, proj, proj, cos4, sin4, rmat, ea, eb)


def _win_kernel(sink_ref, q_ref, k_ref, v_ref, o_ref, *, S, tq, span):
    i = pl.program_id(1)
    q0 = i * tq
    kstart = pl.multiple_of(jnp.clip(q0 - WINDOW, 0, S - span), LANES)
    kwin = k_ref[pl.ds(kstart, span), :]
    vwin = v_ref[pl.ds(kstart, span), :]
    qpos = q0 + lax.broadcasted_iota(jnp.int32, (tq, span), 0)
    kpos = kstart + lax.broadcasted_iota(jnp.int32, (tq, span), 1)
    dist_i = jnp.abs(kpos - qpos)
    valid = dist_i <= WINDOW
    dist = dist_i.astype(F32)
    scale = WIN_HEAD_DIM ** -0.5
    R = WIN_HEADS // WIN_KV_HEADS
    d = WIN_HEAD_DIM
    for g in range(WIN_KV_HEADS):
        kg = kwin[:, g * d:(g + 1) * d]
        vg = vwin[:, g * d:(g + 1) * d]
        qg = jnp.concatenate([q_ref[:, (g * R + r) * d:(g * R + r + 1) * d] for r in range(R)], axis=0)
        sg = lax.dot_general(qg, kg, (((1,), (1,)), ((), ())), preferred_element_type=F32)
        ps = []
        inv = []
        for r in range(R):
            h = g * R + r
            slope = float(2.0 ** (-8.0 * (h + 1) / WIN_HEADS))
            sink = sink_ref[0, h]
            s = sg[r * tq:(r + 1) * tq] * scale - slope * dist
            s = jnp.where(valid, s, NEG_INF)
            m = jnp.maximum(jnp.max(s, axis=-1, keepdims=True), sink)
            p = jnp.exp(s - m)
            denom = jnp.sum(p, axis=-1, keepdims=True) + jnp.exp(sink - m)
            ps.append(p.astype(BF16))
            inv.append(1.0 / denom)
        og = jnp.dot(jnp.concatenate(ps, axis=0), vg, preferred_element_type=F32)
        for r in range(R):
            h = g * R + r
            o_ref[:, h * d:(h + 1) * d] = (og[r * tq:(r + 1) * tq] * inv[r]).astype(BF16)


def window_attention(proj, sink, B, S, tq):
    T = proj.shape[0]
    nq = S // tq
    span = min(tq + 2 * WINDOW, S)
    kern = functools.partial(_win_kernel, S=S, tq=tq, span=span)
    return pl.pallas_call(
        kern,
        grid=(B, nq),
        in_specs=[pl.BlockSpec(memory_space=pltpu.SMEM),
                  pl.BlockSpec((tq, 1024), lambda b, i: (b * nq + i, COL_WQ // 1024)),
                  pl.BlockSpec((S, 256), lambda b, i: (b, COL_WK // 256)),
                  pl.BlockSpec((S, 256), lambda b, i: (b, COL_WV // 256))],
        out_specs=pl.BlockSpec((tq, 1024), lambda b, i: (b * nq + i, 0)),
        out_shape=jax.ShapeDtypeStruct((T, WIN_HEADS * WIN_HEAD_DIM), BF16),
        compiler_params=_params(("parallel", "arbitrary")),
        name="window_attn",
    )(sink.reshape(1, WIN_HEADS), proj, proj, proj)


def _branch_kernel(om_ref, or_ref, ow_ref, g0_ref, g1_ref, g2_ref, b0_ref, b1_ref, b2_ref,
                   wm_ref, wr_ref, ww_ref, o_ref):
    def term(o, g, b, w):
        gate = _sigmoid(g[...].astype(F32) + b[...])
        return gate * jnp.dot(o[...], w[...], preferred_element_type=F32)

    acc = term(om_ref, g0_ref, b0_ref, wm_ref)
    acc = acc + term(or_ref, g1_ref, b1_ref, wr_ref)
    acc = acc + term(ow_ref, g2_ref, b2_ref, ww_ref)
    o_ref[...] = acc.astype(BF16)


def branch_merge(o_mla, o_ret, o_win, proj, b_gate, w_mla, w_ret, w_win, tm, tn):
    T = proj.shape[0]
    D = D_MODEL
    W = BRANCH_WIDTH
    nj = D // tn
    gate0 = COL_GATE // tn
    o_spec = pl.BlockSpec((tm, W), lambda j, i: (i, 0))
    w_spec = pl.BlockSpec((W, tn), lambda j, i: (0, j))

    def g_spec(br):
        return pl.BlockSpec((tm, tn), lambda j, i: (i, gate0 + br * nj + j))

    def b_spec(br):
        return pl.BlockSpec((1, tn), lambda j, i: (0, br * nj + j))

    return pl.pallas_call(
        _branch_kernel,
        grid=(nj, T // tm),
        in_specs=[o_spec, o_spec, o_spec, g_spec(0), g_spec(1), g_spec(2),
                  b_spec(0), b_spec(1), b_spec(2), w_spec, w_spec, w_spec],
        out_specs=pl.BlockSpec((tm, tn), lambda j, i: (i, j)),
        out_shape=jax.ShapeDtypeStruct((T, D), BF16),
        compiler_params=_params(("parallel", "parallel")),
        name="branch_merge",
    )(o_mla, o_ret, o_win, proj, proj, proj, b_gate, b_gate, b_gate, w_mla, w_ret, w_win)


def _mm_res_kernel(a_ref, w_ref, x_ref, g_ref, o_ref):
    y = jnp.dot(a_ref[...], w_ref[...], preferred_element_type=F32)
    o_ref[...] = x_ref[...] + g_ref[...] * y


def matmul_gated_residual(a, w, x, mod3, k_gate, S, tm, tn):
    T, K = a.shape
    N = w.shape[1]
    tpb = S // tm
    nj = N // tn
    return pl.pallas_call(
        _mm_res_kernel,
        grid=(nj, T // tm),
        in_specs=[pl.BlockSpec((tm, K), lambda j, i: (i, 0)),
                  pl.BlockSpec((K, tn), lambda j, i: (0, j)),
                  pl.BlockSpec((tm, tn), lambda j, i: (i, j)),
                  pl.BlockSpec((None, 1, tn), lambda j, i: (i // tpb, 0, k_gate * nj + j))],
        out_specs=pl.BlockSpec((tm, tn), lambda j, i: (i, j)),
        out_shape=jax.ShapeDtypeStruct((T, N), F32),
        compiler_params=_params(("parallel", "parallel")),
        name="out_proj",
    )(a, w, x, mod3)


SUBLANES = 8


def _pack_pair(lo, hi):
    lo_bits = pltpu.bitcast(lo.astype(BF16).astype(F32), jnp.uint32)
    hi_bits = pltpu.bitcast(hi.astype(BF16).astype(F32), jnp.uint32)
    return (lo_bits >> 16) | hi_bits


def _unpack_pair(w):
    lo = pltpu.bitcast(w << 16, F32)
    hi = pltpu.bitcast(w & jnp.uint32(0xFFFF0000), F32)
    return lo, hi


def _store_token_tiles(ref, lead, v):
    rows, d = v.shape
    half = d // 2
    for s in range(SUBLANES):
        w = _pack_pair(v[:, s * LANES:(s + 1) * LANES], v[:, half + s * LANES:half + (s + 1) * LANES])
        ref[lead + (pl.ds(s, rows, stride=SUBLANES), slice(None))] = w


def _load_token_tiles(ref, lead, rows, s):
    return _unpack_pair(ref[lead + (pl.ds(s, rows, stride=SUBLANES), slice(None))])


def _router_kernel(x_ref, g_ref, sc_ref, sh_ref, wh_ref, wl_ref, b_ref, hp_ref, ti_ref, tw_ref):
    h = _rms(x_ref[...], g_ref[...]) * (1.0 + sc_ref[...]) + sh_ref[...]
    _store_token_tiles(hp_ref, (), h)
    h_hi = h.astype(BF16)
    h_lo = (h - h_hi.astype(F32)).astype(BF16)
    wh = wh_ref[...]
    logits = (jnp.dot(h_hi, wh, preferred_element_type=F32)
              + jnp.dot(h_lo, wh, preferred_element_type=F32)
              + jnp.dot(h_hi, wl_ref[...], preferred_element_type=F32)) + b_ref[...]
    lane_i = lax.broadcasted_iota(jnp.int32, logits.shape, 1)
    lane = lane_i.astype(F32)
    vals = []
    idxs = []
    l = logits
    for _ in range(TOP_K):
        m = jnp.max(l, axis=-1, keepdims=True)
        idx = jnp.min(jnp.where(l == m, lane, float(LANES)), axis=-1, keepdims=True)
        vals.append(m)
        idxs.append(idx)
        l = jnp.where(lane == idx, -jnp.inf, l)
    es = [jnp.exp(v - vals[0]) for v in vals]
    tot = es[0] + es[1] + es[2] + es[3]
    ti = jnp.zeros(logits.shape, jnp.int32)
    tw = jnp.zeros(logits.shape, F32)
    for k in range(TOP_K):
        ti = jnp.where(lane_i == k, idxs[k].astype(jnp.int32), ti)
        tw = jnp.where(lane_i == k, es[k] / tot, tw)
    ti_ref[...] = ti
    tw_ref[...] = tw


def ffn_router(x, gain, mod3, k_scale, k_shift, wr_hi, wr_lo, br_pad, S, tm):
    T, D = x.shape
    tpb = S // tm
    return pl.pallas_call(
        _router_kernel,
        grid=(T // tm,),
        in_specs=[pl.BlockSpec((tm, D), lambda i: (i, 0)),
                  pl.BlockSpec((1, D), lambda i: (0, 0)),
                  pl.BlockSpec((None, 1, D), lambda i: (i // tpb, 0, k_scale)),
                  pl.BlockSpec((None, 1, D), lambda i: (i // tpb, 0, k_shift)),
                  pl.BlockSpec((D, LANES), lambda i: (0, 0)),
                  pl.BlockSpec((D, LANES), lambda i: (0, 0)),
                  pl.BlockSpec((1, LANES), lambda i: (0, 0))],
        out_specs=[pl.BlockSpec((tm * SUBLANES, LANES), lambda i: (i, 0)),
                   pl.BlockSpec((tm, LANES), lambda i: (i, 0)),
                   pl.BlockSpec((tm, LANES), lambda i: (i, 0))],
        out_shape=[jax.ShapeDtypeStruct((T * SUBLANES, LANES), jnp.uint32),
                   jax.ShapeDtypeStruct((T, LANES), jnp.int32),
                   jax.ShapeDtypeStruct((T, LANES), F32)],
        compiler_params=_params(("parallel",)),
        name="ffn_router",
    )(x, gain.reshape(1, D), mod3, mod3, wr_hi, wr_lo, br_pad)


def moe_plan(topi, tm):
    T = topi.shape[0]
    E = N_EXPERTS
    n_assign = T * TOP_K
    n_tiles = n_assign // tm + E
    n_slots = n_tiles * tm
    flat_e = topi.reshape(-1)
    onehot = (flat_e[:, None] == jnp.arange(E, dtype=jnp.int32)[None, :]).astype(jnp.int32)
    csum = jnp.cumsum(onehot, axis=0)
    rank = jnp.sum(csum * onehot, axis=1) - 1
    counts = csum[-1]
    tiles_e = (counts + tm - 1) // tm
    cum_tiles = jnp.cumsum(tiles_e)
    tile_start = cum_tiles - tiles_e
    slot = jnp.sum(onehot * tile_start[None, :], axis=1) * tm + rank
    tile_ids = jnp.arange(n_tiles, dtype=jnp.int32)
    tile_e = jnp.minimum(jnp.sum((tile_ids[:, None] >= cum_tiles[None, :]).astype(jnp.int32), axis=1),
                         E - 1).astype(jnp.int32)
    n_used = cum_tiles[-1]
    empty = tiles_e == 0
    n_empty = jnp.sum(empty.astype(jnp.int32))
    fill_tile = jnp.where(empty, n_used + jnp.cumsum(empty.astype(jnp.int32)) - 1, cum_tiles - 1)
    counts_s = jnp.stack([n_used, n_used + n_empty]).astype(jnp.int32)
    return tile_e, counts_s, fill_tile.astype(jnp.int32), (slot * SUBLANES).astype(jnp.int32), n_tiles


def _expert_ffn(h, wgu, bgu, wd, bd):
    F = D_EXPERT
    gu = jnp.dot(h, wgu, preferred_element_type=F32) + bgu
    glu = jnp.minimum(gu[:, :F], SWIGLU_LIMIT)
    lin = jnp.clip(gu[:, F:], -SWIGLU_LIMIT, SWIGLU_LIMIT)
    act = glu * _sigmoid(SWIGLU_ALPHA * glu) * (lin + 1.0)
    return jnp.dot(act.astype(BF16), wd, preferred_element_type=F32) + bd


def _row_tile(first_row):
    if not isinstance(first_row, int):
        first_row = pl.multiple_of(first_row, SUBLANES)
    return pl.ds(first_row, SUBLANES)


def _dispatch_kernel(cnt_ref, fill_ref, slot_ref, hp_ref, hs_hbm, zbuf, rsem, fsem, *, tb, tm, n_tiles):
    i = pl.program_id(0)
    tile_rows = tm * SUBLANES

    def fill(j):
        return pltpu.make_async_copy(
            zbuf, hs_hbm.at[pl.ds(pl.multiple_of(j * tile_rows, tile_rows), tile_rows)], fsem)

    @pl.when(i == 0)
    def _():
        zbuf[...] = jnp.zeros(zbuf.shape, zbuf.dtype)
        for e in range(N_EXPERTS):
            fill(fill_ref[e]).start()

        def start_unused(j, c):
            fill(j).start()
            return c
        lax.fori_loop(cnt_ref[1], n_tiles, start_unused, 0)
        for e in range(N_EXPERTS):
            fill(fill_ref[e]).wait()

        def wait_unused(j, c):
            fill(j).wait()
            return c
        lax.fori_loop(cnt_ref[1], n_tiles, wait_unused, 0)

    for t in range(tb):
        for k in range(TOP_K):
            a = t * TOP_K + k
            pltpu.make_async_copy(hp_ref.at[_row_tile(t * SUBLANES)], hs_hbm.at[_row_tile(slot_ref[0, a])],
                                  rsem).start(priority=a % 2)
    for k in range(TOP_K):
        pltpu.make_async_copy(hp_ref, hs_hbm.at[pl.ds(0, tb * SUBLANES)], rsem).wait()


def moe_dispatch(hp, plan, tb, tm):
    T = hp.shape[0] // SUBLANES
    tile_e, counts_s, fill_tile, slot8, n_tiles = plan
    nb = T // tb
    kern = functools.partial(_dispatch_kernel, tb=tb, tm=tm, n_tiles=n_tiles)
    grid_spec = pltpu.PrefetchScalarGridSpec(
        num_scalar_prefetch=2,
        grid=(nb,),
        in_specs=[pl.BlockSpec((None, 1, tb * TOP_K), lambda i, c, f: (i, 0, 0), memory_space=pltpu.SMEM),
                  pl.BlockSpec((tb * SUBLANES, LANES), lambda i, c, f: (i, 0))],
        out_specs=pl.BlockSpec(memory_space=pltpu.HBM),
        scratch_shapes=[pltpu.VMEM((tm * SUBLANES, LANES), jnp.uint32),
                        pltpu.SemaphoreType.DMA,
                        pltpu.SemaphoreType.DMA])
    return pl.pallas_call(
        kern,
        grid_spec=grid_spec,
        out_shape=jax.ShapeDtypeStruct((n_tiles * tm * SUBLANES, LANES), jnp.uint32),
        compiler_params=_params(("arbitrary",)),
        name="moe_dispatch",
    )(counts_s, fill_tile, slot8.reshape(nb, 1, tb * TOP_K), hp)


def _moe_sparse_kernel(te_ref, cnt_ref, hs_ref, wgu_ref, bgu_ref, wd_ref, bd_ref, ys_ref, h_s, *, tm):
    i = pl.program_id(0)
    half = h_s.shape[-1] // 2

    @pl.when(i < cnt_ref[0])
    def _():
        for s in range(SUBLANES):
            lo, hi = _load_token_tiles(hs_ref, (), tm, s)
            h_s[:, s * LANES:(s + 1) * LANES] = lo.astype(BF16)
            h_s[:, half + s * LANES:half + (s + 1) * LANES] = hi.astype(BF16)
        y = _expert_ffn(h_s[...], wgu_ref[...], bgu_ref[...], wd_ref[...], bd_ref[...])
        _store_token_tiles(ys_ref, (), y)

    @pl.when(i >= cnt_ref[0])
    def _():
        ys_ref[...] = jnp.zeros(ys_ref.shape, ys_ref.dtype)


def moe_sparse(hs, plan, wgu, bgu, wd, bd, tm):
    E, D, F2 = wgu.shape
    F = F2 // 2
    assert D == 2 * SUBLANES * LANES
    tile_e, counts_s, _, _, n_tiles = plan
    blk = (tm * SUBLANES, LANES)
    grid_spec = pltpu.PrefetchScalarGridSpec(
        num_scalar_prefetch=2,
        grid=(n_tiles,),
        in_specs=[pl.BlockSpec(blk, lambda i, te, c: (jnp.minimum(i, c[0] - 1), 0)),
                  pl.BlockSpec((None, D, F2), lambda i, te, c: (te[i], 0, 0)),
                  pl.BlockSpec((None, 1, F2), lambda i, te, c: (te[i], 0, 0)),
                  pl.BlockSpec((None, F, D), lambda i, te, c: (te[i], 0, 0)),
                  pl.BlockSpec((None, 1, D), lambda i, te, c: (te[i], 0, 0))],
        out_specs=pl.BlockSpec(blk, lambda i, te, c: (i, 0)),
        scratch_shapes=[pltpu.VMEM((tm, D), BF16)])
    return pl.pallas_call(
        functools.partial(_moe_sparse_kernel, tm=tm),
        grid_spec=grid_spec,
        out_shape=jax.ShapeDtypeStruct(hs.shape, jnp.uint32),
        compiler_params=_params(("arbitrary",)),
        name="moe_sparse",
    )(tile_e, counts_s, hs, wgu, bgu.reshape(E, 1, F2), wd, bd.reshape(E, 1, D))


def _combine_kernel(slot_cur, slot_nxt, ys_hbm, tw_ref, x_ref, g_ref, o_ref, ybuf, sem, *, tb, nb):
    i = pl.program_id(0)
    cur = i % 2
    half = x_ref.shape[-1] // 2

    def start_row(slot_ref, a, b, priority):
        t, k = divmod(a, TOP_K) if isinstance(a, int) else (a // TOP_K, a % TOP_K)
        pltpu.make_async_copy(ys_hbm.at[_row_tile(slot_ref[0, a])],
                              ybuf.at[b, _row_tile((k * tb + t) * SUBLANES)], sem.at[b]).start(priority=priority)

    @pl.when(i == 0)
    def _():
        def body(a, c):
            start_row(slot_cur, a, 0, 0)
            return c
        lax.fori_loop(0, tb * TOP_K, body, 0, unroll=8)

    @pl.when(i + 1 < nb)
    def _():
        for a in range(tb * TOP_K):
            start_row(slot_nxt, a, 1 - cur, a % 2)

    pltpu.make_async_copy(ys_hbm.at[pl.ds(0, TOP_K * tb * SUBLANES)], ybuf.at[cur], sem.at[cur]).wait()
    tw = tw_ref[...]
    wks = [tw[:, k:k + 1] for k in range(TOP_K)]
    for s in range(SUBLANES):
        acc_lo = None
        acc_hi = None
        for k in range(TOP_K):
            lo, hi = _unpack_pair(ybuf[cur, pl.ds(k * tb * SUBLANES + s, tb, stride=SUBLANES), :])
            acc_lo = wks[k] * lo if acc_lo is None else acc_lo + wks[k] * lo
            acc_hi = wks[k] * hi if acc_hi is None else acc_hi + wks[k] * hi
        c_lo = slice(s * LANES, (s + 1) * LANES)
        c_hi = slice(half + s * LANES, half + (s + 1) * LANES)
        o_ref[:, c_lo] = x_ref[:, c_lo] + g_ref[:, c_lo] * acc_lo
        o_ref[:, c_hi] = x_ref[:, c_hi] + g_ref[:, c_hi] * acc_hi


def moe_combine(ys, slot8, tw, x, mod3, k_gate, S, tb):
    T, D = x.shape
    tpb = S // tb
    nb = T // tb
    slots = slot8.reshape(nb, 1, tb * TOP_K)

    def slot_spec(index_map):
        return pl.BlockSpec((None, 1, tb * TOP_K), index_map, memory_space=pltpu.SMEM)

    return pl.pallas_call(
        functools.partial(_combine_kernel, tb=tb, nb=nb),
        grid=(nb,),
        in_specs=[slot_spec(lambda i: (i, 0, 0)),
                  slot_spec(lambda i: (jnp.minimum(i + 1, nb - 1), 0, 0)),
                  pl.BlockSpec(memory_space=pltpu.HBM),
                  pl.BlockSpec((tb, LANES), lambda i: (i, 0)),
                  pl.BlockSpec((tb, D), lambda i: (i, 0)),
                  pl.BlockSpec((None, 1, D), lambda i: (i // tpb, 0, k_gate))],
        out_specs=pl.BlockSpec((tb, D), lambda i: (i, 0)),
        out_shape=jax.ShapeDtypeStruct((T, D), F32),
        scratch_shapes=[pltpu.VMEM((2, TOP_K * tb * SUBLANES, LANES), jnp.uint32),
                        pltpu.SemaphoreType.DMA((2,))],
        compiler_params=_params(("arbitrary",)),
        name="moe_combine",
    )(slots, slots, ys, tw, x, mod3)


def moe_sparse_layer(hp, ti, tw, wgu, bgu, wd, bd, x, mod3, k_gate, S, tm_moe, tb):
    plan = moe_plan(ti[:, :TOP_K], tm_moe)
    hs = moe_dispatch(hp, plan, tb, tm_moe)
    ys = moe_sparse(hs, plan, wgu, bgu, wd, bd, tm_moe)
    return moe_combine(ys, plan[3], tw, x, mod3, k_gate, S, tb)


def _final_norm_kernel(x_ref, g_ref, o_ref):
    o_ref[...] = _rms(x_ref[...], g_ref[...])


def final_norm(x, gain, tm):
    T, D = x.shape
    return pl.pallas_call(
        _final_norm_kernel,
        grid=(T // tm,),
        in_specs=[pl.BlockSpec((tm, D), lambda i: (i, 0)),
                  pl.BlockSpec((1, D), lambda i: (0, 0))],
        out_specs=pl.BlockSpec((tm, D), lambda i: (i, 0)),
        out_shape=jax.ShapeDtypeStruct((T, D), F32),
        compiler_params=_params(("parallel",)),
        name="final_norm",
    )(x, gain.reshape(1, D))


def _rot_cols(w):
    half = w.shape[-1] // 2
    return jnp.concatenate([-w[..., half:], w[..., :half]], axis=-1)


def _pad_cols(w, width):
    return jnp.pad(w, ((0, 0), (0, width - w.shape[-1])))


def pack_w_in(w_in):
    D = w_in.shape[0]
    sizes = (MLA_Q_LORA, MLA_KV_LORA, MLA_ROPE, RET_HEADS * RET_DK, RET_HEADS * RET_DK,
             RET_HEADS * RET_DV, RET_HEADS * RET_DV, WIN_HEADS * WIN_HEAD_DIM,
             WIN_KV_HEADS * WIN_HEAD_DIM, WIN_KV_HEADS * WIN_HEAD_DIM, N_BRANCH * D_MODEL)
    offs = np.cumsum((0,) + sizes)
    (c_q, c_kv, k_pe, r_q, r_k, r_v, r_g, w_q, w_k, w_v, gate) = [
        w_in[:, offs[i]:offs[i + 1]] for i in range(len(sizes))]
    filler = jnp.zeros((D, COL_GATE - COL_KPER - LANES), w_in.dtype)
    packed = jnp.concatenate(
        [c_q, c_kv, r_q, r_k, r_v, r_g, w_q, w_k, w_v,
         _pad_cols(k_pe, LANES), _pad_cols(_rot_cols(k_pe), LANES), filler, gate], axis=1)
    return packed.astype(BF16)


def pack_w_uq(w_uq):
    H = MLA_HEADS
    w = w_uq.reshape(MLA_Q_LORA, H, MLA_NOPE + MLA_ROPE)
    nope = w[:, :, :MLA_NOPE]
    pe = w[:, :, MLA_NOPE:]
    z = jnp.zeros((MLA_Q_LORA, H, LANES - MLA_ROPE), w.dtype)
    out = jnp.concatenate([nope, pe, z, _rot_cols(pe), z], axis=-1)
    return out.reshape(MLA_Q_LORA, H * 384).astype(BF16)


def pack_w_ukv(w_ukv):
    H = MLA_HEADS
    w = w_ukv.reshape(MLA_KV_LORA, H, 2, MLA_NOPE)
    return w.transpose(0, 2, 1, 3).reshape(MLA_KV_LORA, 2 * H * MLA_NOPE).astype(BF16)


def rope_consts(S):
    inv_freq = 1.0 / (ROPE_THETA ** (jnp.arange(0, MLA_ROPE, 2, dtype=F32) / MLA_ROPE))
    ang = jnp.arange(S, dtype=F32)[:, None] * inv_freq[None, :]
    cos, sin = jnp.cos(ang), jnp.sin(ang)
    z = jnp.zeros((S, LANES - MLA_ROPE), F32)
    cos_pad = jnp.concatenate([cos, cos, z], axis=1)
    sin_pad = jnp.concatenate([sin, sin, z], axis=1)
    cos4 = jnp.concatenate([cos, cos, cos, cos], axis=1)
    sin4 = jnp.concatenate([sin, sin, sin, sin], axis=1)
    return cos_pad, sin_pad, cos4, sin4


def retention_consts():
    k = np.arange(LANES)[:, None]
    j = np.arange(LANES)[None, :]
    jj = j % RET_DK
    half = RET_DK // 2
    rmat = np.where((jj < half) & (k == j + half), -1.0, 0.0) + np.where((jj >= half) & (k == j - half), 1.0, 0.0)
    ea = np.where((k < RET_DK) & ((j == k) | (j == k + RET_DK)), 1.0, 0.0)
    eb = np.where((k >= RET_DK) & ((j == k) | (j == k - RET_DK)), 1.0, 0.0)
    return (jnp.asarray(rmat, BF16), jnp.asarray(ea, BF16), jnp.asarray(eb, BF16))


def _tile(pref, dim):
    return min(pref, dim)


def kernel(x, c, w_ada, b_ada, norm_mix, w_in, b_gate, q_norm, kv_norm, w_uq, w_ukv, ret_decay, sink,
           w_br_mla, w_br_ret, w_br_win, w_out, norm_ffn, w_router, b_router, w_gu, b_gu, w_down,
           b_down, norm_final):
    B, S, D = x.shape
    L = w_ada.shape[0]
    T = B * S
    E = N_EXPERTS
    cos_pad, sin_pad, cos4, sin4 = rope_consts(S)
    rmat, ea, eb = retention_consts()
    mod = ada_mod(c, w_ada, b_ada)
    xt = x.reshape(T, D)
    tm_big = _tile(1024, S)
    for l in range(L):
        mod3 = mod[l].reshape(B, 1, 6 * D)
        proj = norm_mod_matmul(xt, norm_mix[l], mod3, 1, 0, pack_w_in(w_in[l]), S, tm_big, 1536)
        q, k, v = mla_prep(proj, q_norm[l], kv_norm[l], pack_w_uq(w_uq[l]), pack_w_ukv(w_ukv[l]),
                           cos_pad, sin_pad, S, _tile(512, S))
        o_mla = mla_attention(q, k, v, B, S, _tile(512, S), _tile(512, S))
        log_g = -jax.nn.softplus(ret_decay[l].astype(F32))
        o_ret = retention(proj, log_g, cos4, sin4, rmat, ea, eb, B, S, _tile(256, S))
        o_win = window_attention(proj, sink[l], B, S, 128)
        merged = branch_merge(o_mla, o_ret, o_win, proj, b_gate[l].reshape(1, N_BRANCH * D),
                              w_br_mla[l].astype(BF16), w_br_ret[l].astype(BF16),
                              w_br_win[l].astype(BF16), _tile(512, S), 1024)
        xt = matmul_gated_residual(merged, w_out[l].astype(BF16), xt, mod3, 2, S, tm_big, 1024)
        wr = jnp.pad(w_router[l], ((0, 0), (0, LANES - E)))
        wr_hi = wr.astype(BF16)
        wr_lo = (wr - wr_hi.astype(F32)).astype(BF16)
        br_pad = jnp.concatenate([b_router[l].astype(F32), jnp.full((LANES - E,), NEG_INF, F32)]).reshape(1, LANES)
        hp, ti, tw = ffn_router(xt, norm_ffn[l], mod3, 4, 3, wr_hi, wr_lo, br_pad, S, _tile(512, S))
        xt = moe_sparse_layer(hp, ti, tw, w_gu[l].astype(BF16), b_gu[l], w_down[l].astype(BF16),
                              b_down[l], xt, mod3, 5, S, _tile(512, S), _tile(256, S))
    out = final_norm(xt, norm_final, tm_big)
    return out.reshape(B, S, D)
```

```python
import functools

import jax
import jax.numpy as jnp
import numpy as np
from jax import lax
from jax.experimental import pallas as pl
from jax.experimental.pallas import tpu as pltpu

F32 = jnp.float32
BF16 = jnp.bfloat16

D_MODEL = 2048
MLA_HEADS = 8
MLA_Q_LORA = 512
MLA_KV_LORA = 512
MLA_NOPE = 128
MLA_ROPE = 64
MLA_V = 128
RET_HEADS = 8
RET_DK = 64
RET_DV = 128
WIN_HEADS = 16
WIN_KV_HEADS = 4
WIN_HEAD_DIM = 64
WINDOW = 128
N_BRANCH = 3
BRANCH_WIDTH = 1024
N_EXPERTS = 32
TOP_K = 4
D_EXPERT = 768
SWIGLU_LIMIT = 7.0
SWIGLU_ALPHA = 1.702
ROPE_THETA = 10000.0
EPS = 1e-6
NEG_INF = -1e30
LOG2_E = 1.4426950408889634

LANES = 128
VMEM_LIMIT_BYTES = 56 * 1024 * 1024

COL_CQ = 0
COL_CKV = 512
COL_RQ = 1024
COL_RK = 1536
COL_RV = 2048
COL_RG = 3072
COL_WQ = 4096
COL_WK = 5120
COL_WV = 5376
COL_KPE = 5632
COL_KPER = 5760
COL_GATE = 6144
IN_WIDTH_PACKED = COL_GATE + N_BRANCH * D_MODEL


def _params(sem):
    return pltpu.CompilerParams(dimension_semantics=sem, vmem_limit_bytes=VMEM_LIMIT_BYTES)


def _sigmoid(z):
    return 1.0 / (1.0 + jnp.exp(-z))


def _rms(x, gain):
    return x * lax.rsqrt(jnp.mean(x * x, axis=-1, keepdims=True) + EPS) * gain


def _ada_kernel(c_ref, w_ref, b_ref, o_ref):
    c = c_ref[...]
    ca = (c * _sigmoid(c)).astype(BF16)
    o_ref[...] = jnp.dot(ca, w_ref[...].astype(BF16), preferred_element_type=F32) + b_ref[...]


def ada_mod(c, w_ada, b_ada):
    L, D, N = w_ada.shape
    B = c.shape[0]
    tn = 1536
    return pl.pallas_call(
        _ada_kernel,
        grid=(L, N // tn),
        in_specs=[pl.BlockSpec((B, D), lambda l, j: (0, 0)),
                  pl.BlockSpec((None, D, tn), lambda l, j: (l, 0, j)),
                  pl.BlockSpec((None, 1, tn), lambda l, j: (l, 0, j))],
        out_specs=pl.BlockSpec((None, B, tn), lambda l, j: (l, 0, j)),
        out_shape=jax.ShapeDtypeStruct((L, B, N), F32),
        compiler_params=_params(("parallel", "parallel")),
        name="ada_mod",
    )(c, w_ada, b_ada.reshape(L, 1, N))


def _nmm_kernel(x_ref, g_ref, sc_ref, sh_ref, w_ref, o_ref, h_ref):
    @pl.when(pl.program_id(1) == 0)
    def _():
        h_ref[...] = (_rms(x_ref[...], g_ref[...]) * (1.0 + sc_ref[...]) + sh_ref[...]).astype(BF16)

    o_ref[...] = jnp.dot(h_ref[...], w_ref[...], preferred_element_type=F32).astype(o_ref.dtype)


def norm_mod_matmul(x, gain, mod3, k_scale, k_shift, w, S, tm, tn):
    T, D = x.shape
    N = w.shape[1]
    tpb = S // tm
    return pl.pallas_call(
        _nmm_kernel,
        grid=(T // tm, N // tn),
        in_specs=[pl.BlockSpec((tm, D), lambda i, j: (i, 0)),
                  pl.BlockSpec((1, D), lambda i, j: (0, 0)),
                  pl.BlockSpec((None, 1, D), lambda i, j: (i // tpb, 0, k_scale)),
                  pl.BlockSpec((None, 1, D), lambda i, j: (i // tpb, 0, k_shift)),
                  pl.BlockSpec((D, tn), lambda i, j: (0, j))],
        out_specs=pl.BlockSpec((tm, tn), lambda i, j: (i, j)),
        out_shape=jax.ShapeDtypeStruct((T, N), BF16),
        scratch_shapes=[pltpu.VMEM((tm, D), BF16)],
        compiler_params=_params(("parallel", "arbitrary")),
        name="in_proj",
    )(x, gain.reshape(1, D), mod3, mod3, w)


def _mla_prep_kernel(cq_ref, ckv_ref, kpe_ref, kper_ref, qn_ref, kvn_ref, wq_ref, wkv_ref,
                     cos_ref, sin_ref, q_ref, k_ref, v_ref):
    H = MLA_HEADS
    scale = (MLA_NOPE + MLA_ROPE) ** -0.5 * LOG2_E
    cos = cos_ref[...]
    sin = sin_ref[...]
    yq = _rms(cq_ref[...].astype(F32), qn_ref[...]).astype(BF16)
    for h in range(H):
        z = jnp.dot(yq, wq_ref[:, h * 384:(h + 1) * 384], preferred_element_type=F32)
        q_ref[:, h * 256:h * 256 + 128] = (z[:, :128] * scale).astype(BF16)
        q_ref[:, h * 256 + 128:(h + 1) * 256] = (
            (z[:, 128:256] * cos + z[:, 256:384] * sin) * scale).astype(BF16)
    ykv = _rms(ckv_ref[...].astype(F32), kvn_ref[...]).astype(BF16)
    kpe = (kpe_ref[...].astype(F32) * cos + kper_ref[...].astype(F32) * sin).astype(BF16)
    zk = jnp.dot(ykv, wkv_ref[:, :H * MLA_NOPE], preferred_element_type=F32)
    for h in range(H):
        k_ref[:, h * 256:h * 256 + 128] = zk[:, h * 128:(h + 1) * 128].astype(BF16)
        k_ref[:, h * 256 + 128:(h + 1) * 256] = kpe
    v_ref[...] = jnp.dot(ykv, wkv_ref[:, H * MLA_NOPE:], preferred_element_type=F32).astype(BF16)


def mla_prep(proj, q_norm, kv_norm, wq_p, wkv_p, cos_pad, sin_pad, S, tm):
    T = proj.shape[0]
    H = MLA_HEADS
    tpb = S // tm
    return pl.pallas_call(
        _mla_prep_kernel,
        grid=(T // tm,),
        in_specs=[pl.BlockSpec((tm, 512), lambda i: (i, COL_CQ // 512)),
                  pl.BlockSpec((tm, 512), lambda i: (i, COL_CKV // 512)),
                  pl.BlockSpec((tm, 128), lambda i: (i, COL_KPE // 128)),
                  pl.BlockSpec((tm, 128), lambda i: (i, COL_KPER // 128)),
                  pl.BlockSpec((1, 512), lambda i: (0, 0)),
                  pl.BlockSpec((1, 512), lambda i: (0, 0)),
                  pl.BlockSpec((512, H * 384), lambda i: (0, 0)),
                  pl.BlockSpec((512, H * 256), lambda i: (0, 0)),
                  pl.BlockSpec((tm, 128), lambda i: (i % tpb, 0)),
                  pl.BlockSpec((tm, 128), lambda i: (i % tpb, 0))],
        out_specs=[pl.BlockSpec((tm, H * 256), lambda i: (i, 0)),
                   pl.BlockSpec((tm, H * 256), lambda i: (i, 0)),
                   pl.BlockSpec((tm, H * 128), lambda i: (i, 0))],
        out_shape=[jax.ShapeDtypeStruct((T, H * 256), BF16),
                   jax.ShapeDtypeStruct((T, H * 256), BF16),
                   jax.ShapeDtypeStruct((T, H * 128), BF16)],
        compiler_params=_params(("parallel",)),
        name="mla_prep",
    )(proj, proj, proj, proj, q_norm.reshape(1, 512), kv_norm.reshape(1, 512), wq_p, wkv_p,
      cos_pad, sin_pad)


def _mla_attn_kernel(q_ref, k_ref, v_ref, o_ref, *, tk):
    q = q_ref[...]
    S = k_ref.shape[0]
    n_chunks = S // tk

    def scores(c):
        return lax.dot_general(q, k_ref[c * tk:(c + 1) * tk, :], (((1,), (1,)), ((), ())),
                               preferred_element_type=F32)

    m = l = acc = None
    s_next = scores(0)
    for c in range(n_chunks):
        s = s_next
        if c + 1 < n_chunks:
            s_next = scores(c + 1)
        mc = jnp.max(s, axis=-1, keepdims=True)
        m_new = mc if m is None else jnp.maximum(m, mc)
        p = jnp.exp2(s - m_new)
        pv = jnp.dot(p.astype(BF16), v_ref[c * tk:(c + 1) * tk, :], preferred_element_type=F32)
        ps = jnp.sum(p, axis=-1, keepdims=True)
        if m is None:
            l, acc = ps, pv
        else:
            alpha = jnp.exp2(m - m_new)
            l = alpha * l + ps
            acc = alpha * acc + pv
        m = m_new
    o_ref[...] = (acc / l).astype(BF16)


def mla_attention(q, k, v, B, S, tq, tk):
    T = q.shape[0]
    H = MLA_HEADS
    nq = S // tq
    return pl.pallas_call(
        functools.partial(_mla_attn_kernel, tk=tk),
        grid=(B, H, nq),
        in_specs=[pl.BlockSpec((tq, 256), lambda b, h, i: (b * nq + i, h)),
                  pl.BlockSpec((S, 256), lambda b, h, i: (b, h)),
                  pl.BlockSpec((S, 128), lambda b, h, i: (b, h))],
        out_specs=pl.BlockSpec((tq, 128), lambda b, h, i: (b * nq + i, h)),
        out_shape=jax.ShapeDtypeStruct((T, H * MLA_V), BF16),
        compiler_params=_params(("parallel", "parallel", "arbitrary")),
        name="mla_attn",
    )(q, k, v)


def _ret_kernel(lg_ref, q_ref, k_ref, v_ref, g_ref, cos_ref, sin_ref, rmat_ref, ea_ref, eb_ref,
                o_ref, qd_s, kd_s, kv_s, st_s, *, C, N):
    pair = pl.program_id(1)
    rmat = rmat_ref[...]
    emats = (ea_ref[...], eb_ref[...])
    heads = range(2)

    def chunk(n):
        return pl.ds(pl.multiple_of(n * C, C), C)

    def rope_body(n, carry):
        sl = chunk(n)
        cos = cos_ref[sl, :]
        sin = sin_ref[sl, :]
        qx = q_ref[sl, :]
        kx = k_ref[sl, :]
        qr = qx.astype(F32) * cos + jnp.dot(qx, rmat, preferred_element_type=F32) * sin
        kr = kx.astype(F32) * cos + jnp.dot(kx, rmat, preferred_element_type=F32) * sin
        qb = qr.astype(BF16)
        kb = (kr * (RET_DK ** -0.5)).astype(BF16)
        for hh in heads:
            qd_s[hh, sl, :] = jnp.dot(qb, emats[hh], preferred_element_type=F32).astype(BF16)
            kd_s[hh, sl, :] = jnp.dot(kb, emats[hh], preferred_element_type=F32).astype(BF16)
        return carry

    lax.fori_loop(0, N, rope_body, 0, unroll=2)

    rowi = lax.broadcasted_iota(jnp.int32, (C, LANES), 0).astype(F32)
    left = lax.broadcasted_iota(jnp.int32, (C, LANES), 1) < RET_DK
    dij = (lax.broadcasted_iota(jnp.int32, (C, C), 0)
           - lax.broadcasted_iota(jnp.int32, (C, C), 1)).astype(F32)
    top = lax.broadcasted_iota(jnp.int32, (2 * RET_DK, RET_DV), 0) < RET_DK
    vcols = [slice(hh * RET_DV, (hh + 1) * RET_DV) for hh in heads]
    q_w, k_w, d_intra, dec = [], [], [], []
    for hh in heads:
        lgf = lg_ref[0, pair * 2 + hh]
        lgb = lg_ref[1, pair * 2 + hh]
        q_w.append(jnp.exp(jnp.where(left, lgf * (rowi + 1.0), lgb * (C - rowi))))
        k_w.append(jnp.exp(jnp.where(left, lgf * (C - 1.0 - rowi), lgb * rowi)))
        d_intra.append(jnp.exp(jnp.where(dij >= 0, lgf * dij, -lgb * dij)))
        dec.append(jnp.exp(jnp.where(top, lgf * C, lgb * C)))

    def kv_body(n, carry):
        sl = chunk(n)
        for hh in heads:
            kdw_t = (kd_s[hh, sl, :].astype(F32) * k_w[hh]).T.astype(BF16)
            kv_s[hh, n] = jnp.dot(kdw_t, v_ref[sl, vcols[hh]], preferred_element_type=F32)
        return carry

    lax.fori_loop(0, N, kv_body, 0, unroll=2)

    def fwd_body(n, cf):
        out = []
        for hh in heads:
            st_s[hh, n, 0:RET_DK, :] = cf[hh][0:RET_DK]
            out.append(cf[hh] * dec[hh] + kv_s[hh, n])
        return tuple(out)

    zero = jnp.zeros((2 * RET_DK, RET_DV), F32)
    lax.fori_loop(0, N, fwd_body, (zero, zero))

    def bwd_body(t, cb):
        n = N - 1 - t
        out = []
        for hh in heads:
            st_s[hh, n, RET_DK:2 * RET_DK, :] = cb[hh][RET_DK:2 * RET_DK]
            out.append(cb[hh] * dec[hh] + kv_s[hh, n])
        return tuple(out)

    lax.fori_loop(0, N, bwd_body, (zero, zero))

    def out_body(n, carry):
        sl = chunk(n)
        for hh in heads:
            qd = qd_s[hh, sl, :]
            qs = jnp.where(left, qd, jnp.zeros_like(qd))
            s = lax.dot_general(qs, kd_s[hh, sl, :], (((1,), (1,)), ((), ())),
                                preferred_element_type=F32)
            sd = (s * d_intra[hh]).astype(BF16)
            o = jnp.dot(sd, v_ref[sl, vcols[hh]], preferred_element_type=F32)
            o = o + jnp.dot((qd.astype(F32) * q_w[hh]).astype(BF16), st_s[hh, n].astype(BF16),
                            preferred_element_type=F32)
            mu = jnp.mean(o, axis=-1, keepdims=True)
            dlt = o - mu
            var = jnp.mean(dlt * dlt, axis=-1, keepdims=True)
            on = dlt * lax.rsqrt(var + EPS)
            g = g_ref[sl, vcols[hh]].astype(F32)
            o_ref[sl, vcols[hh]] = (g * _sigmoid(g) * on).astype(BF16)
        return carry

    lax.fori_loop(0, N, out_body, 0, unroll=2)


def retention(proj, log_g, cos4, sin4, rmat, ea, eb, B, S, C):
    T = proj.shape[0]
    N = S // C
    npair = RET_HEADS // 2
    kern = functools.partial(_ret_kernel, C=C, N=N)
    return pl.pallas_call(
        kern,
        grid=(B, npair),
        in_specs=[pl.BlockSpec(memory_space=pltpu.SMEM),
                  pl.BlockSpec((S, 128), lambda b, p: (b, COL_RQ // 128 + p)),
                  pl.BlockSpec((S, 128), lambda b, p: (b, COL_RK // 128 + p)),
                  pl.BlockSpec((S, 256), lambda b, p: (b, COL_RV // 256 + p)),
                  pl.BlockSpec((S, 256), lambda b, p: (b, COL_RG // 256 + p)),
                  pl.BlockSpec((S, 128), lambda b, p: (0, 0)),
                  pl.BlockSpec((S, 128), lambda b, p: (0, 0)),
                  pl.BlockSpec((128, 128), lambda b, p: (0, 0)),
                  pl.BlockSpec((128, 128), lambda b, p: (0, 0)),
                  pl.BlockSpec((128, 128), lambda b, p: (0, 0))],
        out_specs=pl.BlockSpec((S, 256), lambda b, p: (b, p)),
        out_shape=jax.ShapeDtypeStruct((T, RET_HEADS * RET_DV), BF16),
        scratch_shapes=[pltpu.VMEM((2, S, 128), BF16),
                        pltpu.VMEM((2, S, 128), BF16),
                        pltpu.VMEM((2, N, 2 * RET_DK, RET_DV), F32),
                        pltpu.VMEM((2, N, 2 * RET_DK, RET_DV), F32)],
        compiler_params=_params(("parallel", "arbitrary")),
        name="retention",
    )(log_g, proj, proj, proj, proj, cos4, sin4, rmat, ea, eb)


def _win_kernel(sink_ref, q_ref, k_ref, v_ref, o_ref, *, S, tq, span):
    i = pl.program_id(1)
    q0 = i * tq
    kstart = pl.multiple_of(jnp.clip(q0 - WINDOW, 0, S - span), LANES)
    kwin = k_ref[pl.ds(kstart, span), :]
    vwin = v_ref[pl.ds(kstart, span), :]
    qpos = q0 + lax.broadcasted_iota(jnp.int32, (tq, span), 0)
    kpos = kstart + lax.broadcasted_iota(jnp.int32, (tq, span), 1)
    dist_i = jnp.abs(kpos - qpos)
    valid = dist_i <= WINDOW
    dist = dist_i.astype(F32)
    scale = WIN_HEAD_DIM ** -0.5
    R = WIN_HEADS // WIN_KV_HEADS
    d = WIN_HEAD_DIM
    for g in range(WIN_KV_HEADS):
        kg = kwin[:, g * d:(g + 1) * d]
        vg = vwin[:, g * d:(g + 1) * d]
        qg = jnp.concatenate([q_ref[:, (g * R + r) * d:(g * R + r + 1) * d] for r in range(R)], axis=0)
        qg = (qg.astype(F32) * (scale * LOG2_E)).astype(BF16)
        sg = lax.dot_general(qg, kg, (((1,), (1,)), ((), ())), preferred_element_type=F32)
        ps = []
        inv = []
        for r in range(R):
            h = g * R + r
            slope = float(2.0 ** (-8.0 * (h + 1) / WIN_HEADS) * LOG2_E)
            sink = sink_ref[0, h] * LOG2_E
            s = sg[r * tq:(r + 1) * tq] - slope * dist
            s = jnp.where(valid, s, NEG_INF)
            m = jnp.maximum(jnp.max(s, axis=-1, keepdims=True), sink)
            p = jnp.exp2(s - m)
            denom = jnp.sum(p, axis=-1, keepdims=True) + jnp.exp2(sink - m)
            ps.append(p.astype(BF16))
            inv.append(1.0 / denom)
        og = jnp.dot(jnp.concatenate(ps, axis=0), vg, preferred_element_type=F32)
        for r in range(R):
            h = g * R + r
            o_ref[:, h * d:(h + 1) * d] = (og[r * tq:(r + 1) * tq] * inv[r]).astype(BF16)


def window_attention(proj, sink, B, S, tq):
    T = proj.shape[0]
    nq = S // tq
    span = min(tq + 2 * WINDOW, S)
    kern = functools.partial(_win_kernel, S=S, tq=tq, span=span)
    return pl.pallas_call(
        kern,
        grid=(B, nq),
        in_specs=[pl.BlockSpec(memory_space=pltpu.SMEM),
                  pl.BlockSpec((tq, 1024), lambda b, i: (b * nq + i, COL_WQ // 1024)),
                  pl.BlockSpec((S, 256), lambda b, i: (b, COL_WK // 256)),
                  pl.BlockSpec((S, 256), lambda b, i: (b, COL_WV // 256))],
        out_specs=pl.BlockSpec((tq, 1024), lambda b, i: (b * nq + i, 0)),
        out_shape=jax.ShapeDtypeStruct((T, WIN_HEADS * WIN_HEAD_DIM), BF16),
        compiler_params=_params(("parallel", "arbitrary")),
        name="window_attn",
    )(sink.reshape(1, WIN_HEADS), proj, proj, proj)


def _branch_kernel(om_ref, or_ref, ow_ref, g0_ref, g1_ref, g2_ref, b0_ref, b1_ref, b2_ref,
                   wm_ref, wr_ref, ww_ref, o_ref):
    def term(o, g, b, w):
        gate = _sigmoid(g[...].astype(F32) + b[...])
        return gate * jnp.dot(o[...], w[...], preferred_element_type=F32)

    acc = term(om_ref, g0_ref, b0_ref, wm_ref)
    acc = acc + term(or_ref, g1_ref, b1_ref, wr_ref)
    acc = acc + term(ow_ref, g2_ref, b2_ref, ww_ref)
    o_ref[...] = acc.astype(BF16)


def branch_merge(o_mla, o_ret, o_win, proj, b_gate, w_mla, w_ret, w_win, tm, tn):
    T = proj.shape[0]
    D = D_MODEL
    W = BRANCH_WIDTH
    nj = D // tn
    gate0 = COL_GATE // tn
    o_spec = pl.BlockSpec((tm, W), lambda j, i: (i, 0))
    w_spec = pl.BlockSpec((W, tn), lambda j, i: (0, j))

    def g_spec(br):
        return pl.BlockSpec((tm, tn), lambda j, i: (i, gate0 + br * nj + j))

    def b_spec(br):
        return pl.BlockSpec((1, tn), lambda j, i: (0, br * nj + j))

    return pl.pallas_call(
        _branch_kernel,
        grid=(nj, T // tm),
        in_specs=[o_spec, o_spec, o_spec, g_spec(0), g_spec(1), g_spec(2),
                  b_spec(0), b_spec(1), b_spec(2), w_spec, w_spec, w_spec],
        out_specs=pl.BlockSpec((tm, tn), lambda j, i: (i, j)),
        out_shape=jax.ShapeDtypeStruct((T, D), BF16),
        compiler_params=_params(("parallel", "parallel")),
        name="branch_merge",
    )(o_mla, o_ret, o_win, proj, proj, proj, b_gate, b_gate, b_gate, w_mla, w_ret, w_win)


def _mm_res_kernel(a_ref, w_ref, x_ref, g_ref, o_ref):
    y = jnp.dot(a_ref[...], w_ref[...], preferred_element_type=F32)
    o_ref[...] = x_ref[...] + g_ref[...] * y


def matmul_gated_residual(a, w, x, mod3, k_gate, S, tm, tn):
    T, K = a.shape
    N = w.shape[1]
    tpb = S // tm
    nj = N // tn
    return pl.pallas_call(
        _mm_res_kernel,
        grid=(nj, T // tm),
        in_specs=[pl.BlockSpec((tm, K), lambda j, i: (i, 0)),
                  pl.BlockSpec((K, tn), lambda j, i: (0, j)),
                  pl.BlockSpec((tm, tn), lambda j, i: (i, j)),
                  pl.BlockSpec((None, 1, tn), lambda j, i: (i // tpb, 0, k_gate * nj + j))],
        out_specs=pl.BlockSpec((tm, tn), lambda j, i: (i, j)),
        out_shape=jax.ShapeDtypeStruct((T, N), F32),
        compiler_params=_params(("parallel", "parallel")),
        name="out_proj",
    )(a, w, x, mod3)


SUBLANES = 8


def _pack_pair(lo, hi):
    lo_bits = pltpu.bitcast(lo.astype(BF16).astype(F32), jnp.uint32)
    hi_bits = pltpu.bitcast(hi.astype(BF16).astype(F32), jnp.uint32)
    return (lo_bits >> 16) | hi_bits


def _unpack_pair(w):
    lo = pltpu.bitcast(w << 16, F32)
    hi = pltpu.bitcast(w & jnp.uint32(0xFFFF0000), F32)
    return lo, hi


def _store_token_tiles(ref, lead, v):
    rows, d = v.shape
    half = d // 2
    for s in range(SUBLANES):
        w = _pack_pair(v[:, s * LANES:(s + 1) * LANES], v[:, half + s * LANES:half + (s + 1) * LANES])
        ref[lead + (pl.ds(s, rows, stride=SUBLANES), slice(None))] = w


def _load_token_tiles(ref, lead, rows, s):
    return _unpack_pair(ref[lead + (pl.ds(s, rows, stride=SUBLANES), slice(None))])


def _router_kernel(x_ref, g_ref, sc_ref, sh_ref, wh_ref, wl_ref, b_ref, hp_ref, ti_ref, tw_ref):
    h = _rms(x_ref[...], g_ref[...]) * (1.0 + sc_ref[...]) + sh_ref[...]
    _store_token_tiles(hp_ref, (), h)
    h_hi = h.astype(BF16)
    h_lo = (h - h_hi.astype(F32)).astype(BF16)
    wh = wh_ref[...]
    logits = (jnp.dot(h_hi, wh, preferred_element_type=F32)
              + jnp.dot(h_lo, wh, preferred_element_type=F32)
              + jnp.dot(h_hi, wl_ref[...], preferred_element_type=F32)) + b_ref[...]
    lane_i = lax.broadcasted_iota(jnp.int32, logits.shape, 1)
    lane = lane_i.astype(F32)
    vals = []
    idxs = []
    l = logits
    for _ in range(TOP_K):
        m = jnp.max(l, axis=-1, keepdims=True)
        idx = jnp.min(jnp.where(l == m, lane, float(LANES)), axis=-1, keepdims=True)
        vals.append(m)
        idxs.append(idx)
        l = jnp.where(lane == idx, -jnp.inf, l)
    es = [jnp.exp(v - vals[0]) for v in vals]
    tot = es[0] + es[1] + es[2] + es[3]
    ti = jnp.zeros(logits.shape, jnp.int32)
    tw = jnp.zeros(logits.shape, F32)
    for k in range(TOP_K):
        ti = jnp.where(lane_i == k, idxs[k].astype(jnp.int32), ti)
        tw = jnp.where(lane_i == k, es[k] / tot, tw)
    ti_ref[...] = ti
    tw_ref[...] = tw


def ffn_router(x, gain, mod3, k_scale, k_shift, wr_hi, wr_lo, br_pad, S, tm):
    T, D = x.shape
    tpb = S // tm
    return pl.pallas_call(
        _router_kernel,
        grid=(T // tm,),
        in_specs=[pl.BlockSpec((tm, D), lambda i: (i, 0)),
                  pl.BlockSpec((1, D), lambda i: (0, 0)),
                  pl.BlockSpec((None, 1, D), lambda i: (i // tpb, 0, k_scale)),
                  pl.BlockSpec((None, 1, D), lambda i: (i // tpb, 0, k_shift)),
                  pl.BlockSpec((D, LANES), lambda i: (0, 0)),
                  pl.BlockSpec((D, LANES), lambda i: (0, 0)),
                  pl.BlockSpec((1, LANES), lambda i: (0, 0))],
        out_specs=[pl.BlockSpec((tm * SUBLANES, LANES), lambda i: (i, 0)),
                   pl.BlockSpec((tm, LANES), lambda i: (i, 0)),
                   pl.BlockSpec((tm, LANES), lambda i: (i, 0))],
        out_shape=[jax.ShapeDtypeStruct((T * SUBLANES, LANES), jnp.uint32),
                   jax.ShapeDtypeStruct((T, LANES), jnp.int32),
                   jax.ShapeDtypeStruct((T, LANES), F32)],
        compiler_params=_params(("parallel",)),
        name="ffn_router",
    )(x, gain.reshape(1, D), mod3, mod3, wr_hi, wr_lo, br_pad)


def moe_plan(topi, tm):
    T = topi.shape[0]
    E = N_EXPERTS
    n_assign = T * TOP_K
    n_tiles = n_assign // tm + E
    n_slots = n_tiles * tm
    flat_e = topi.reshape(-1)
    onehot = (flat_e[:, None] == jnp.arange(E, dtype=jnp.int32)[None, :]).astype(jnp.int32)
    csum = jnp.cumsum(onehot, axis=0)
    rank = jnp.sum(csum * onehot, axis=1) - 1
    counts = csum[-1]
    tiles_e = (counts + tm - 1) // tm
    cum_tiles = jnp.cumsum(tiles_e)
    tile_start = cum_tiles - tiles_e
    slot = jnp.sum(onehot * tile_start[None, :], axis=1) * tm + rank
    tile_ids = jnp.arange(n_tiles, dtype=jnp.int32)
    tile_e = jnp.minimum(jnp.sum((tile_ids[:, None] >= cum_tiles[None, :]).astype(jnp.int32), axis=1),
                         E - 1).astype(jnp.int32)
    n_used = cum_tiles[-1]
    empty = tiles_e == 0
    n_empty = jnp.sum(empty.astype(jnp.int32))
    fill_tile = jnp.where(empty, n_used + jnp.cumsum(empty.astype(jnp.int32)) - 1, cum_tiles - 1)
    counts_s = jnp.stack([n_used, n_used + n_empty]).astype(jnp.int32)
    return tile_e, counts_s, fill_tile.astype(jnp.int32), (slot * SUBLANES).astype(jnp.int32), n_tiles


def _expert_ffn(h, wgu, bgu, wd, bd):
    F = D_EXPERT
    gu = jnp.dot(h, wgu, preferred_element_type=F32) + bgu
    glu = jnp.minimum(gu[:, :F], SWIGLU_LIMIT)
    lin = jnp.clip(gu[:, F:], -SWIGLU_LIMIT, SWIGLU_LIMIT)
    act = glu * _sigmoid(SWIGLU_ALPHA * glu) * (lin + 1.0)
    return jnp.dot(act.astype(BF16), wd, preferred_element_type=F32) + bd


def _row_tile(first_row):
    if not isinstance(first_row, int):
        first_row = pl.multiple_of(first_row, SUBLANES)
    return pl.ds(first_row, SUBLANES)


def _dispatch_kernel(cnt_ref, fill_ref, slot_ref, hp_ref, hs_hbm, zbuf, rsem, fsem, *, tb, tm, n_tiles):
    i = pl.program_id(0)
    tile_rows = tm * SUBLANES

    def fill(j):
        return pltpu.make_async_copy(
            zbuf, hs_hbm.at[pl.ds(pl.multiple_of(j * tile_rows, tile_rows), tile_rows)], fsem)

    @pl.when(i == 0)
    def _():
        zbuf[...] = jnp.zeros(zbuf.shape, zbuf.dtype)
        for e in range(N_EXPERTS):
            fill(fill_ref[e]).start()

        def start_unused(j, c):
            fill(j).start()
            return c
        lax.fori_loop(cnt_ref[1], n_tiles, start_unused, 0)
        for e in range(N_EXPERTS):
            fill(fill_ref[e]).wait()

        def wait_unused(j, c):
            fill(j).wait()
            return c
        lax.fori_loop(cnt_ref[1], n_tiles, wait_unused, 0)

    for t in range(tb):
        for k in range(TOP_K):
            a = t * TOP_K + k
            pltpu.make_async_copy(hp_ref.at[_row_tile(t * SUBLANES)], hs_hbm.at[_row_tile(slot_ref[0, a])],
                                  rsem).start(priority=a % 2)
    for k in range(TOP_K):
        pltpu.make_async_copy(hp_ref, hs_hbm.at[pl.ds(0, tb * SUBLANES)], rsem).wait()


def moe_dispatch(hp, plan, tb, tm):
    T = hp.shape[0] // SUBLANES
    tile_e, counts_s, fill_tile, slot8, n_tiles = plan
    nb = T // tb
    kern = functools.partial(_dispatch_kernel, tb=tb, tm=tm, n_tiles=n_tiles)
    grid_spec = pltpu.PrefetchScalarGridSpec(
        num_scalar_prefetch=2,
        grid=(nb,),
        in_specs=[pl.BlockSpec((None, 1, tb * TOP_K), lambda i, c, f: (i, 0, 0), memory_space=pltpu.SMEM),
                  pl.BlockSpec((tb * SUBLANES, LANES), lambda i, c, f: (i, 0))],
        out_specs=pl.BlockSpec(memory_space=pltpu.HBM),
        scratch_shapes=[pltpu.VMEM((tm * SUBLANES, LANES), jnp.uint32),
                        pltpu.SemaphoreType.DMA,
                        pltpu.SemaphoreType.DMA])
    return pl.pallas_call(
        kern,
        grid_spec=grid_spec,
        out_shape=jax.ShapeDtypeStruct((n_tiles * tm * SUBLANES, LANES), jnp.uint32),
        compiler_params=_params(("arbitrary",)),
        name="moe_dispatch",
    )(counts_s, fill_tile, slot8.reshape(nb, 1, tb * TOP_K), hp)


def _moe_sparse_kernel(te_ref, cnt_ref, hs_ref, wgu_ref, bgu_ref, wd_ref, bd_ref, ys_ref, h_s, *, tm):
    i = pl.program_id(0)
    half = h_s.shape[-1] // 2

    @pl.when(i < cnt_ref[0])
    def _():
        for s in range(SUBLANES):
            lo, hi = _load_token_tiles(hs_ref, (), tm, s)
            h_s[:, s * LANES:(s + 1) * LANES] = lo.astype(BF16)
            h_s[:, half + s * LANES:half + (s + 1) * LANES] = hi.astype(BF16)
        y = _expert_ffn(h_s[...], wgu_ref[...], bgu_ref[...], wd_ref[...], bd_ref[...])
        _store_token_tiles(ys_ref, (), y)

    @pl.when(i >= cnt_ref[0])
    def _():
        ys_ref[...] = jnp.zeros(ys_ref.shape, ys_ref.dtype)


def moe_sparse(hs, plan, wgu, bgu, wd, bd, tm):
    E, D, F2 = wgu.shape
    F = F2 // 2
    assert D == 2 * SUBLANES * LANES
    tile_e, counts_s, _, _, n_tiles = plan
    blk = (tm * SUBLANES, LANES)
    grid_spec = pltpu.PrefetchScalarGridSpec(
        num_scalar_prefetch=2,
        grid=(n_tiles,),
        in_specs=[pl.BlockSpec(blk, lambda i, te, c: (jnp.minimum(i, c[0] - 1), 0)),
                  pl.BlockSpec((None, D, F2), lambda i, te, c: (te[i], 0, 0)),
                  pl.BlockSpec((None, 1, F2), lambda i, te, c: (te[i], 0, 0)),
                  pl.BlockSpec((None, F, D), lambda i, te, c: (te[i], 0, 0)),
                  pl.BlockSpec((None, 1, D), lambda i, te, c: (te[i], 0, 0))],
        out_specs=pl.BlockSpec(blk, lambda i, te, c: (i, 0)),
        scratch_shapes=[pltpu.VMEM((tm, D), BF16)])
    return pl.pallas_call(
        functools.partial(_moe_sparse_kernel, tm=tm),
        grid_spec=grid_spec,
        out_shape=jax.ShapeDtypeStruct(hs.shape, jnp.uint32),
        compiler_params=_params(("arbitrary",)),
        name="moe_sparse",
    )(tile_e, counts_s, hs, wgu, bgu.reshape(E, 1, F2), wd, bd.reshape(E, 1, D))


def _combine_kernel(slot_cur, slot_nxt, ys_hbm, tw_ref, x_ref, g_ref, o_ref, ybuf, sem, *, tb, nb):
    i = pl.program_id(0)
    cur = i % 2
    half = x_ref.shape[-1] // 2

    def start_row(slot_ref, a, b, priority):
        t, k = divmod(a, TOP_K) if isinstance(a, int) else (a // TOP_K, a % TOP_K)
        pltpu.make_async_copy(ys_hbm.at[_row_tile(slot_ref[0, a])],
                              ybuf.at[b, _row_tile((k * tb + t) * SUBLANES)], sem.at[b]).start(priority=priority)

    @pl.when(i == 0)
    def _():
        def body(a, c):
            start_row(slot_cur, a, 0, 0)
            return c
        lax.fori_loop(0, tb * TOP_K, body, 0, unroll=8)

    @pl.when(i + 1 < nb)
    def _():
        for a in range(tb * TOP_K):
            start_row(slot_nxt, a, 1 - cur, a % 2)

    pltpu.make_async_copy(ys_hbm.at[pl.ds(0, TOP_K * tb * SUBLANES)], ybuf.at[cur], sem.at[cur]).wait()
    tw = tw_ref[...]
    wks = [tw[:, k:k + 1] for k in range(TOP_K)]
    for s in range(SUBLANES):
        acc_lo = None
        acc_hi = None
        for k in range(TOP_K):
            lo, hi = _unpack_pair(ybuf[cur, pl.ds(k * tb * SUBLANES + s, tb, stride=SUBLANES), :])
            acc_lo = wks[k] * lo if acc_lo is None else acc_lo + wks[k] * lo
            acc_hi = wks[k] * hi if acc_hi is None else acc_hi + wks[k] * hi
        c_lo = slice(s * LANES, (s + 1) * LANES)
        c_hi = slice(half + s * LANES, half + (s + 1) * LANES)
        o_ref[:, c_lo] = x_ref[:, c_lo] + g_ref[:, c_lo] * acc_lo
        o_ref[:, c_hi] = x_ref[:, c_hi] + g_ref[:, c_hi] * acc_hi


def moe_combine(ys, slot8, tw, x, mod3, k_gate, S, tb):
    T, D = x.shape
    tpb = S // tb
    nb = T // tb
    slots = slot8.reshape(nb, 1, tb * TOP_K)

    def slot_spec(index_map):
        return pl.BlockSpec((None, 1, tb * TOP_K), index_map, memory_space=pltpu.SMEM)

    return pl.pallas_call(
        functools.partial(_combine_kernel, tb=tb, nb=nb),
        grid=(nb,),
        in_specs=[slot_spec(lambda i: (i, 0, 0)),
                  slot_spec(lambda i: (jnp.minimum(i + 1, nb - 1), 0, 0)),
                  pl.BlockSpec(memory_space=pltpu.HBM),
                  pl.BlockSpec((tb, LANES), lambda i: (i, 0)),
                  pl.BlockSpec((tb, D), lambda i: (i, 0)),
                  pl.BlockSpec((None, 1, D), lambda i: (i // tpb, 0, k_gate))],
        out_specs=pl.BlockSpec((tb, D), lambda i: (i, 0)),
        out_shape=jax.ShapeDtypeStruct((T, D), F32),
        scratch_shapes=[pltpu.VMEM((2, TOP_K * tb * SUBLANES, LANES), jnp.uint32),
                        pltpu.SemaphoreType.DMA((2,))],
        compiler_params=_params(("arbitrary",)),
        name="moe_combine",
    )(slots, slots, ys, tw, x, mod3)


def moe_sparse_layer(hp, ti, tw, wgu, bgu, wd, bd, x, mod3, k_gate, S, tm_moe, tb):
    plan = moe_plan(ti[:, :TOP_K], tm_moe)
    hs = moe_dispatch(hp, plan, tb, tm_moe)
    ys = moe_sparse(hs, plan, wgu, bgu, wd, bd, tm_moe)
    return moe_combine(ys, plan[3], tw, x, mod3, k_gate, S, tb)


def _final_norm_kernel(x_ref, g_ref, o_ref):
    o_ref[...] = _rms(x_ref[...], g_ref[...])


def final_norm(x, gain, tm):
    T, D = x.shape
    return pl.pallas_call(
        _final_norm_kernel,
        grid=(T // tm,),
        in_specs=[pl.BlockSpec((tm, D), lambda i: (i, 0)),
                  pl.BlockSpec((1, D), lambda i: (0, 0))],
        out_specs=pl.BlockSpec((tm, D), lambda i: (i, 0)),
        out_shape=jax.ShapeDtypeStruct((T, D), F32),
        compiler_params=_params(("parallel",)),
        name="final_norm",
    )(x, gain.reshape(1, D))


def _rot_cols(w):
    half = w.shape[-1] // 2
    return jnp.concatenate([-w[..., half:], w[..., :half]], axis=-1)


def _pad_cols(w, width):
    return jnp.pad(w, ((0, 0), (0, width - w.shape[-1])))


def pack_w_in(w_in):
    D = w_in.shape[0]
    sizes = (MLA_Q_LORA, MLA_KV_LORA, MLA_ROPE, RET_HEADS * RET_DK, RET_HEADS * RET_DK,
             RET_HEADS * RET_DV, RET_HEADS * RET_DV, WIN_HEADS * WIN_HEAD_DIM,
             WIN_KV_HEADS * WIN_HEAD_DIM, WIN_KV_HEADS * WIN_HEAD_DIM, N_BRANCH * D_MODEL)
    offs = np.cumsum((0,) + sizes)
    (c_q, c_kv, k_pe, r_q, r_k, r_v, r_g, w_q, w_k, w_v, gate) = [
        w_in[:, offs[i]:offs[i + 1]] for i in range(len(sizes))]
    filler = jnp.zeros((D, COL_GATE - COL_KPER - LANES), w_in.dtype)
    packed = jnp.concatenate(
        [c_q, c_kv, r_q, r_k, r_v, r_g, w_q, w_k, w_v,
         _pad_cols(k_pe, LANES), _pad_cols(_rot_cols(k_pe), LANES), filler, gate], axis=1)
    return packed.astype(BF16)


def pack_w_uq(w_uq):
    H = MLA_HEADS
    w = w_uq.reshape(MLA_Q_LORA, H, MLA_NOPE + MLA_ROPE)
    nope = w[:, :, :MLA_NOPE]
    pe = w[:, :, MLA_NOPE:]
    z = jnp.zeros((MLA_Q_LORA, H, LANES - MLA_ROPE), w.dtype)
    out = jnp.concatenate([nope, pe, z, _rot_cols(pe), z], axis=-1)
    return out.reshape(MLA_Q_LORA, H * 384).astype(BF16)


def pack_w_ukv(w_ukv):
    H = MLA_HEADS
    w = w_ukv.reshape(MLA_KV_LORA, H, 2, MLA_NOPE)
    return w.transpose(0, 2, 1, 3).reshape(MLA_KV_LORA, 2 * H * MLA_NOPE).astype(BF16)


def rope_consts(S):
    inv_freq = 1.0 / (ROPE_THETA ** (jnp.arange(0, MLA_ROPE, 2, dtype=F32) / MLA_ROPE))
    ang = jnp.arange(S, dtype=F32)[:, None] * inv_freq[None, :]
    cos, sin = jnp.cos(ang), jnp.sin(ang)
    z = jnp.zeros((S, LANES - MLA_ROPE), F32)
    cos_pad = jnp.concatenate([cos, cos, z], axis=1)
    sin_pad = jnp.concatenate([sin, sin, z], axis=1)
    cos4 = jnp.concatenate([cos, cos, cos, cos], axis=1)
    sin4 = jnp.concatenate([sin, sin, sin, sin], axis=1)
    return cos_pad, sin_pad, cos4, sin4


def retention_consts():
    k = np.arange(LANES)[:, None]
    j = np.arange(LANES)[None, :]
    jj = j % RET_DK
    half = RET_DK // 2
    rmat = np.where((jj < half) & (k == j + half), -1.0, 0.0) + np.where((jj >= half) & (k == j - half), 1.0, 0.0)
    ea = np.where((k < RET_DK) & ((j == k) | (j == k + RET_DK)), 1.0, 0.0)
    eb = np.where((k >= RET_DK) & ((j == k) | (j == k - RET_DK)), 1.0, 0.0)
    return (jnp.asarray(rmat, BF16), jnp.asarray(ea, BF16), jnp.asarray(eb, BF16))


def _tile(pref, dim):
    return min(pref, dim)


def kernel(x, c, w_ada, b_ada, norm_mix, w_in, b_gate, q_norm, kv_norm, w_uq, w_ukv, ret_decay, sink,
           w_br_mla, w_br_ret, w_br_win, w_out, norm_ffn, w_router, b_router, w_gu, b_gu, w_down,
           b_down, norm_final):
    B, S, D = x.shape
    L = w_ada.shape[0]
    T = B * S
    E = N_EXPERTS
    cos_pad, sin_pad, cos4, sin4 = rope_consts(S)
    rmat, ea, eb = retention_consts()
    mod = ada_mod(c, w_ada, b_ada)
    xt = x.reshape(T, D)
    tm_big = _tile(1024, S)
    for l in range(L):
        mod3 = mod[l].reshape(B, 1, 6 * D)
        proj = norm_mod_matmul(xt, norm_mix[l], mod3, 1, 0, pack_w_in(w_in[l]), S, tm_big, 1536)
        q, k, v = mla_prep(proj, q_norm[l], kv_norm[l], pack_w_uq(w_uq[l]), pack_w_ukv(w_ukv[l]),
                           cos_pad, sin_pad, S, _tile(512, S))
        o_mla = mla_attention(q, k, v, B, S, _tile(512, S), _tile(512, S))
        log_g = -jax.nn.softplus(ret_decay[l].astype(F32))
        o_ret = retention(proj, log_g, cos4, sin4, rmat, ea, eb, B, S, _tile(256, S))
        o_win = window_attention(proj, sink[l], B, S, 128)
        merged = branch_merge(o_mla, o_ret, o_win, proj, b_gate[l].reshape(1, N_BRANCH * D),
                              w_br_mla[l].astype(BF16), w_br_ret[l].astype(BF16),
                              w_br_win[l].astype(BF16), _tile(512, S), 1024)
        xt = matmul_gated_residual(merged, w_out[l].astype(BF16), xt, mod3, 2, S, tm_big, 1024)
        wr = jnp.pad(w_router[l], ((0, 0), (0, LANES - E)))
        wr_hi = wr.astype(BF16)
        wr_lo = (wr - wr_hi.astype(F32)).astype(BF16)
        br_pad = jnp.concatenate([b_router[l].astype(F32), jnp.full((LANES - E,), NEG_INF, F32)]).reshape(1, LANES)
        hp, ti, tw = ffn_router(xt, norm_ffn[l], mod3, 4, 3, wr_hi, wr_lo, br_pad, S, _tile(512, S))
        xt = moe_sparse_layer(hp, ti, tw, w_gu[l].astype(BF16), b_gu[l], w_down[l].astype(BF16),
                              b_down[l], xt, mod3, 5, S, _tile(512, S), _tile(256, S))
    out = final_norm(xt, norm_final, tm_big)
    return out.reshape(B, S, D)
```

```python
import functools

import jax
import jax.numpy as jnp
import numpy as np
from jax import lax
from jax.experimental import pallas as pl
from jax.experimental.pallas import tpu as pltpu

F32 = jnp.float32
BF16 = jnp.bfloat16

D_MODEL = 2048
MLA_HEADS = 8
MLA_Q_LORA = 512
MLA_KV_LORA = 512
MLA_NOPE = 128
MLA_ROPE = 64
MLA_V = 128
RET_HEADS = 8
RET_DK = 64
RET_DV = 128
WIN_HEADS = 16
WIN_KV_HEADS = 4
WIN_HEAD_DIM = 64
WINDOW = 128
N_BRANCH = 3
BRANCH_WIDTH = 1024
N_EXPERTS = 32
TOP_K = 4
D_EXPERT = 768
SWIGLU_LIMIT = 7.0
SWIGLU_ALPHA = 1.702
ROPE_THETA = 10000.0
EPS = 1e-6
NEG_INF = -1e30
LOG2_E = 1.4426950408889634

LANES = 128
VMEM_LIMIT_BYTES = 56 * 1024 * 1024

COL_CQ = 0
COL_CKV = 512
COL_RQ = 1024
COL_RK = 1536
COL_RV = 2048
COL_RG = 3072
COL_WQ = 4096
COL_WK = 5120
COL_WV = 5376
COL_KPE = 5632
COL_KPER = 5760
COL_GATE = 6144
IN_WIDTH_PACKED = COL_GATE + N_BRANCH * D_MODEL


def _params(sem):
    return pltpu.CompilerParams(dimension_semantics=sem, vmem_limit_bytes=VMEM_LIMIT_BYTES)


def _sigmoid(z):
    return 1.0 / (1.0 + jnp.exp(-z))


def _rms(x, gain):
    return x * lax.rsqrt(jnp.mean(x * x, axis=-1, keepdims=True) + EPS) * gain


def _ada_kernel(c_ref, w_ref, b_ref, o_ref):
    c = c_ref[...]
    ca = (c * _sigmoid(c)).astype(BF16)
    o_ref[...] = jnp.dot(ca, w_ref[...].astype(BF16), preferred_element_type=F32) + b_ref[...]


def ada_mod(c, w_ada, b_ada):
    L, D, N = w_ada.shape
    B = c.shape[0]
    tn = 1536
    return pl.pallas_call(
        _ada_kernel,
        grid=(L, N // tn),
        in_specs=[pl.BlockSpec((B, D), lambda l, j: (0, 0)),
                  pl.BlockSpec((None, D, tn), lambda l, j: (l, 0, j)),
                  pl.BlockSpec((None, 1, tn), lambda l, j: (l, 0, j))],
        out_specs=pl.BlockSpec((None, B, tn), lambda l, j: (l, 0, j)),
        out_shape=jax.ShapeDtypeStruct((L, B, N), F32),
        compiler_params=_params(("parallel", "parallel")),
        name="ada_mod",
    )(c, w_ada, b_ada.reshape(L, 1, N))


def _nmm_kernel(x_ref, g_ref, sc_ref, sh_ref, w_ref, o_ref, h_ref):
    @pl.when(pl.program_id(1) == 0)
    def _():
        h_ref[...] = (_rms(x_ref[...], g_ref[...]) * (1.0 + sc_ref[...]) + sh_ref[...]).astype(BF16)

    o_ref[...] = jnp.dot(h_ref[...], w_ref[...], preferred_element_type=F32).astype(o_ref.dtype)


def norm_mod_matmul(x, gain, mod3, k_scale, k_shift, w, S, tm, tn):
    T, D = x.shape
    N = w.shape[1]
    tpb = S // tm
    return pl.pallas_call(
        _nmm_kernel,
        grid=(T // tm, N // tn),
        in_specs=[pl.BlockSpec((tm, D), lambda i, j: (i, 0)),
                  pl.BlockSpec((1, D), lambda i, j: (0, 0)),
                  pl.BlockSpec((None, 1, D), lambda i, j: (i // tpb, 0, k_scale)),
                  pl.BlockSpec((None, 1, D), lambda i, j: (i // tpb, 0, k_shift)),
                  pl.BlockSpec((D, tn), lambda i, j: (0, j))],
        out_specs=pl.BlockSpec((tm, tn), lambda i, j: (i, j)),
        out_shape=jax.ShapeDtypeStruct((T, N), BF16),
        scratch_shapes=[pltpu.VMEM((tm, D), BF16)],
        compiler_params=_params(("parallel", "arbitrary")),
        name="in_proj",
    )(x, gain.reshape(1, D), mod3, mod3, w)


def _mla_prep_kernel(cq_ref, ckv_ref, kpe_ref, kper_ref, qn_ref, kvn_ref, wq_ref, wkv_ref,
                     cos_ref, sin_ref, q_ref, k_ref, v_ref):
    H = MLA_HEADS
    scale = (MLA_NOPE + MLA_ROPE) ** -0.5 * LOG2_E
    cos = cos_ref[...]
    sin = sin_ref[...]
    yq = _rms(cq_ref[...].astype(F32), qn_ref[...]).astype(BF16)
    for h in range(H):
        z = jnp.dot(yq, wq_ref[:, h * 384:(h + 1) * 384], preferred_element_type=F32)
        q_ref[:, h * 256:h * 256 + 128] = (z[:, :128] * scale).astype(BF16)
        q_ref[:, h * 256 + 128:(h + 1) * 256] = (
            (z[:, 128:256] * cos + z[:, 256:384] * sin) * scale).astype(BF16)
    ykv = _rms(ckv_ref[...].astype(F32), kvn_ref[...]).astype(BF16)
    kpe = (kpe_ref[...].astype(F32) * cos + kper_ref[...].astype(F32) * sin).astype(BF16)
    zk = jnp.dot(ykv, wkv_ref[:, :H * MLA_NOPE], preferred_element_type=F32)
    for h in range(H):
        k_ref[:, h * 256:h * 256 + 128] = zk[:, h * 128:(h + 1) * 128].astype(BF16)
        k_ref[:, h * 256 + 128:(h + 1) * 256] = kpe
    v_ref[...] = jnp.dot(ykv, wkv_ref[:, H * MLA_NOPE:], preferred_element_type=F32).astype(BF16)


def mla_prep(proj, q_norm, kv_norm, wq_p, wkv_p, cos_pad, sin_pad, S, tm):
    T = proj.shape[0]
    H = MLA_HEADS
    tpb = S // tm
    return pl.pallas_call(
        _mla_prep_kernel,
        grid=(T // tm,),
        in_specs=[pl.BlockSpec((tm, 512), lambda i: (i, COL_CQ // 512)),
                  pl.BlockSpec((tm, 512), lambda i: (i, COL_CKV // 512)),
                  pl.BlockSpec((tm, 128), lambda i: (i, COL_KPE // 128)),
                  pl.BlockSpec((tm, 128), lambda i: (i, COL_KPER // 128)),
                  pl.BlockSpec((1, 512), lambda i: (0, 0)),
                  pl.BlockSpec((1, 512), lambda i: (0, 0)),
                  pl.BlockSpec((512, H * 384), lambda i: (0, 0)),
                  pl.BlockSpec((512, H * 256), lambda i: (0, 0)),
                  pl.BlockSpec((tm, 128), lambda i: (i % tpb, 0)),
                  pl.BlockSpec((tm, 128), lambda i: (i % tpb, 0))],
        out_specs=[pl.BlockSpec((tm, H * 256), lambda i: (i, 0)),
                   pl.BlockSpec((tm, H * 256), lambda i: (i, 0)),
                   pl.BlockSpec((tm, H * 128), lambda i: (i, 0))],
        out_shape=[jax.ShapeDtypeStruct((T, H * 256), BF16),
                   jax.ShapeDtypeStruct((T, H * 256), BF16),
                   jax.ShapeDtypeStruct((T, H * 128), BF16)],
        compiler_params=_params(("parallel",)),
        name="mla_prep",
    )(proj, proj, proj, proj, q_norm.reshape(1, 512), kv_norm.reshape(1, 512), wq_p, wkv_p,
      cos_pad, sin_pad)


def _mla_attn_kernel(q_ref, k_ref, v_ref, o_ref, *, tk):
    S = k_ref.shape[0]
    n_chunks = S // tk
    heads = range(MLA_HEADS_PER_STEP)
    qs = [q_ref[:, hh * 256:(hh + 1) * 256] for hh in heads]

    def scores(hh, c):
        return lax.dot_general(qs[hh], k_ref[c * tk:(c + 1) * tk, hh * 256:(hh + 1) * 256],
                               (((1,), (1,)), ((), ())), preferred_element_type=F32)

    m = [None for _ in heads]
    l = [None for _ in heads]
    acc = [None for _ in heads]
    s_next = [scores(hh, 0) for hh in heads]
    for c in range(n_chunks):
        for hh in heads:
            s = s_next[hh]
            if c + 1 < n_chunks:
                s_next[hh] = scores(hh, c + 1)
            mc = jnp.max(s, axis=-1, keepdims=True)
            m_new = mc if m[hh] is None else jnp.maximum(m[hh], mc)
            p = jnp.exp2(s - m_new)
            pv = jnp.dot(p.astype(BF16), v_ref[c * tk:(c + 1) * tk, hh * MLA_V:(hh + 1) * MLA_V],
                         preferred_element_type=F32)
            ps = jnp.sum(p, axis=-1, keepdims=True)
            if m[hh] is None:
                l[hh], acc[hh] = ps, pv
            else:
                alpha = jnp.exp2(m[hh] - m_new)
                l[hh] = alpha * l[hh] + ps
                acc[hh] = alpha * acc[hh] + pv
            m[hh] = m_new
    for hh in heads:
        o_ref[:, hh * MLA_V:(hh + 1) * MLA_V] = (acc[hh] / l[hh]).astype(BF16)


MLA_HEADS_PER_STEP = 2


def mla_attention(q, k, v, B, S, tq, tk):
    T = q.shape[0]
    H = MLA_HEADS
    G = MLA_HEADS_PER_STEP
    nq = S // tq
    return pl.pallas_call(
        functools.partial(_mla_attn_kernel, tk=tk),
        grid=(B, H // G, nq),
        in_specs=[pl.BlockSpec((tq, G * 256), lambda b, h, i: (b * nq + i, h)),
                  pl.BlockSpec((S, G * 256), lambda b, h, i: (b, h)),
                  pl.BlockSpec((S, G * 128), lambda b, h, i: (b, h))],
        out_specs=pl.BlockSpec((tq, G * 128), lambda b, h, i: (b * nq + i, h)),
        out_shape=jax.ShapeDtypeStruct((T, H * MLA_V), BF16),
        compiler_params=_params(("parallel", "parallel", "arbitrary")),
        name="mla_attn",
    )(q, k, v)


def _ret_kernel(lg_ref, q_ref, k_ref, v_ref, g_ref, cos_ref, sin_ref, rmat_ref, ea_ref, eb_ref,
                o_ref, qd_s, kd_s, kv_s, st_s, *, C, N):
    pair = pl.program_id(1)
    rmat = rmat_ref[...]
    emats = (ea_ref[...], eb_ref[...])
    heads = range(2)

    def chunk(n):
        return pl.ds(pl.multiple_of(n * C, C), C)

    def rope_body(n, carry):
        sl = chunk(n)
        cos = cos_ref[sl, :]
        sin = sin_ref[sl, :]
        qx = q_ref[sl, :]
        kx = k_ref[sl, :]
        qr = qx.astype(F32) * cos + jnp.dot(qx, rmat, preferred_element_type=F32) * sin
        kr = kx.astype(F32) * cos + jnp.dot(kx, rmat, preferred_element_type=F32) * sin
        qb = qr.astype(BF16)
        kb = (kr * (RET_DK ** -0.5)).astype(BF16)
        for hh in heads:
            qd_s[hh, sl, :] = jnp.dot(qb, emats[hh], preferred_element_type=F32).astype(BF16)
            kd_s[hh, sl, :] = jnp.dot(kb, emats[hh], preferred_element_type=F32).astype(BF16)
        return carry

    lax.fori_loop(0, N, rope_body, 0, unroll=2)

    rowi = lax.broadcasted_iota(jnp.int32, (C, LANES), 0).astype(F32)
    left = lax.broadcasted_iota(jnp.int32, (C, LANES), 1) < RET_DK
    dij = (lax.broadcasted_iota(jnp.int32, (C, C), 0)
           - lax.broadcasted_iota(jnp.int32, (C, C), 1)).astype(F32)
    top = lax.broadcasted_iota(jnp.int32, (2 * RET_DK, RET_DV), 0) < RET_DK
    vcols = [slice(hh * RET_DV, (hh + 1) * RET_DV) for hh in heads]
    q_w, k_w, d_intra, dec = [], [], [], []
    for hh in heads:
        lgf = lg_ref[0, pair * 2 + hh]
        lgb = lg_ref[1, pair * 2 + hh]
        q_w.append(jnp.exp(jnp.where(left, lgf * (rowi + 1.0), lgb * (C - rowi))))
        k_w.append(jnp.exp(jnp.where(left, lgf * (C - 1.0 - rowi), lgb * rowi)))
        d_intra.append(jnp.exp(jnp.where(dij >= 0, lgf * dij, -lgb * dij)))
        dec.append(jnp.exp(jnp.where(top, lgf * C, lgb * C)))

    def kv_body(n, carry):
        sl = chunk(n)
        for hh in heads:
            kdw_t = (kd_s[hh, sl, :].astype(F32) * k_w[hh]).T.astype(BF16)
            kv_s[hh, n] = jnp.dot(kdw_t, v_ref[sl, vcols[hh]], preferred_element_type=F32)
        return carry

    lax.fori_loop(0, N, kv_body, 0, unroll=2)

    def fwd_body(n, cf):
        out = []
        for hh in heads:
            st_s[hh, n, 0:RET_DK, :] = cf[hh][0:RET_DK]
            out.append(cf[hh] * dec[hh] + kv_s[hh, n])
        return tuple(out)

    zero = jnp.zeros((2 * RET_DK, RET_DV), F32)
    lax.fori_loop(0, N, fwd_body, (zero, zero))

    def bwd_body(t, cb):
        n = N - 1 - t
        out = []
        for hh in heads:
            st_s[hh, n, RET_DK:2 * RET_DK, :] = cb[hh][RET_DK:2 * RET_DK]
            out.append(cb[hh] * dec[hh] + kv_s[hh, n])
        return tuple(out)

    lax.fori_loop(0, N, bwd_body, (zero, zero))

    def out_body(n, carry):
        sl = chunk(n)
        for hh in heads:
            qd = qd_s[hh, sl, :]
            qs = jnp.where(left, qd, jnp.zeros_like(qd))
            s = lax.dot_general(qs, kd_s[hh, sl, :], (((1,), (1,)), ((), ())),
                                preferred_element_type=F32)
            sd = (s * d_intra[hh]).astype(BF16)
            o = jnp.dot(sd, v_ref[sl, vcols[hh]], preferred_element_type=F32)
            o = o + jnp.dot((qd.astype(F32) * q_w[hh]).astype(BF16), st_s[hh, n].astype(BF16),
                            preferred_element_type=F32)
            mu = jnp.mean(o, axis=-1, keepdims=True)
            dlt = o - mu
            var = jnp.mean(dlt * dlt, axis=-1, keepdims=True)
            on = dlt * lax.rsqrt(var + EPS)
            g = g_ref[sl, vcols[hh]].astype(F32)
            o_ref[sl, vcols[hh]] = (g * _sigmoid(g) * on).astype(BF16)
        return carry

    lax.fori_loop(0, N, out_body, 0, unroll=2)


def retention(proj, log_g, cos4, sin4, rmat, ea, eb, B, S, C):
    T = proj.shape[0]
    N = S // C
    npair = RET_HEADS // 2
    kern = functools.partial(_ret_kernel, C=C, N=N)
    return pl.pallas_call(
        kern,
        grid=(B, npair),
        in_specs=[pl.BlockSpec(memory_space=pltpu.SMEM),
                  pl.BlockSpec((S, 128), lambda b, p: (b, COL_RQ // 128 + p)),
                  pl.BlockSpec((S, 128), lambda b, p: (b, COL_RK // 128 + p)),
                  pl.BlockSpec((S, 256), lambda b, p: (b, COL_RV // 256 + p)),
                  pl.BlockSpec((S, 256), lambda b, p: (b, COL_RG // 256 + p)),
                  pl.BlockSpec((S, 128), lambda b, p: (0, 0)),
                  pl.BlockSpec((S, 128), lambda b, p: (0, 0)),
                  pl.BlockSpec((128, 128), lambda b, p: (0, 0)),
                  pl.BlockSpec((128, 128), lambda b, p: (0, 0)),
                  pl.BlockSpec((128, 128), lambda b, p: (0, 0))],
        out_specs=pl.BlockSpec((S, 256), lambda b, p: (b, p)),
        out_shape=jax.ShapeDtypeStruct((T, RET_HEADS * RET_DV), BF16),
        scratch_shapes=[pltpu.VMEM((2, S, 128), BF16),
                        pltpu.VMEM((2, S, 128), BF16),
                        pltpu.VMEM((2, N, 2 * RET_DK, RET_DV), F32),
                        pltpu.VMEM((2, N, 2 * RET_DK, RET_DV), F32)],
        compiler_params=_params(("parallel", "arbitrary")),
        name="retention",
    )(log_g, proj, proj, proj, proj, cos4, sin4, rmat, ea, eb)


def _win_kernel(sink_ref, q_ref, k_ref, v_ref, o_ref, *, S, tq, span):
    i = pl.program_id(1)
    q0 = i * tq
    kstart = pl.multiple_of(jnp.clip(q0 - WINDOW, 0, S - span), LANES)
    kwin = k_ref[pl.ds(kstart, span), :]
    vwin = v_ref[pl.ds(kstart, span), :]
    qpos = q0 + lax.broadcasted_iota(jnp.int32, (tq, span), 0)
    kpos = kstart + lax.broadcasted_iota(jnp.int32, (tq, span), 1)
    dist_i = jnp.abs(kpos - qpos)
    valid = dist_i <= WINDOW
    dist = dist_i.astype(F32)
    scale = WIN_HEAD_DIM ** -0.5
    R = WIN_HEADS // WIN_KV_HEADS
    d = WIN_HEAD_DIM
    for g in range(WIN_KV_HEADS):
        kg = kwin[:, g * d:(g + 1) * d]
        vg = vwin[:, g * d:(g + 1) * d]
        qg = jnp.concatenate([q_ref[:, (g * R + r) * d:(g * R + r + 1) * d] for r in range(R)], axis=0)
        qg = (qg.astype(F32) * (scale * LOG2_E)).astype(BF16)
        sg = lax.dot_general(qg, kg, (((1,), (1,)), ((), ())), preferred_element_type=F32)
        ps = []
        inv = []
        for r in range(R):
            h = g * R + r
            slope = float(2.0 ** (-8.0 * (h + 1) / WIN_HEADS) * LOG2_E)
            sink = sink_ref[0, h] * LOG2_E
            s = sg[r * tq:(r + 1) * tq] - slope * dist
            s = jnp.where(valid, s, NEG_INF)
            m = jnp.maximum(jnp.max(s, axis=-1, keepdims=True), sink)
            p = jnp.exp2(s - m)
            denom = jnp.sum(p, axis=-1, keepdims=True) + jnp.exp2(sink - m)
            ps.append(p.astype(BF16))
            inv.append(1.0 / denom)
        og = jnp.dot(jnp.concatenate(ps, axis=0), vg, preferred_element_type=F32)
        for r in range(R):
            h = g * R + r
            o_ref[:, h * d:(h + 1) * d] = (og[r * tq:(r + 1) * tq] * inv[r]).astype(BF16)


def window_attention(proj, sink, B, S, tq):
    T = proj.shape[0]
    nq = S // tq
    span = min(tq + 2 * WINDOW, S)
    kern = functools.partial(_win_kernel, S=S, tq=tq, span=span)
    return pl.pallas_call(
        kern,
        grid=(B, nq),
        in_specs=[pl.BlockSpec(memory_space=pltpu.SMEM),
                  pl.BlockSpec((tq, 1024), lambda b, i: (b * nq + i, COL_WQ // 1024)),
                  pl.BlockSpec((S, 256), lambda b, i: (b, COL_WK // 256)),
                  pl.BlockSpec((S, 256), lambda b, i: (b, COL_WV // 256))],
        out_specs=pl.BlockSpec((tq, 1024), lambda b, i: (b * nq + i, 0)),
        out_shape=jax.ShapeDtypeStruct((T, WIN_HEADS * WIN_HEAD_DIM), BF16),
        compiler_params=_params(("parallel", "arbitrary")),
        name="window_attn",
    )(sink.reshape(1, WIN_HEADS), proj, proj, proj)


def _branch_kernel(om_ref, or_ref, ow_ref, g0_ref, g1_ref, g2_ref, b0_ref, b1_ref, b2_ref,
                   wm_ref, wr_ref, ww_ref, o_ref):
    def term(o, g, b, w):
        gate = _sigmoid(g[...].astype(F32) + b[...])
        return gate * jnp.dot(o[...], w[...], preferred_element_type=F32)

    acc = term(om_ref, g0_ref, b0_ref, wm_ref)
    acc = acc + term(or_ref, g1_ref, b1_ref, wr_ref)
    acc = acc + term(ow_ref, g2_ref, b2_ref, ww_ref)
    o_ref[...] = acc.astype(BF16)


def branch_merge(o_mla, o_ret, o_win, proj, b_gate, w_mla, w_ret, w_win, tm, tn):
    T = proj.shape[0]
    D = D_MODEL
    W = BRANCH_WIDTH
    nj = D // tn
    gate0 = COL_GATE // tn
    o_spec = pl.BlockSpec((tm, W), lambda j, i: (i, 0))
    w_spec = pl.BlockSpec((W, tn), lambda j, i: (0, j))

    def g_spec(br):
        return pl.BlockSpec((tm, tn), lambda j, i: (i, gate0 + br * nj + j))

    def b_spec(br):
        return pl.BlockSpec((1, tn), lambda j, i: (0, br * nj + j))

    return pl.pallas_call(
        _branch_kernel,
        grid=(nj, T // tm),
        in_specs=[o_spec, o_spec, o_spec, g_spec(0), g_spec(1), g_spec(2),
                  b_spec(0), b_spec(1), b_spec(2), w_spec, w_spec, w_spec],
        out_specs=pl.BlockSpec((tm, tn), lambda j, i: (i, j)),
        out_shape=jax.ShapeDtypeStruct((T, D), BF16),
        compiler_params=_params(("parallel", "parallel")),
        name="branch_merge",
    )(o_mla, o_ret, o_win, proj, proj, proj, b_gate, b_gate, b_gate, w_mla, w_ret, w_win)


def _mm_res_kernel(a_ref, w_ref, x_ref, g_ref, o_ref):
    y = jnp.dot(a_ref[...], w_ref[...], preferred_element_type=F32)
    o_ref[...] = x_ref[...] + g_ref[...] * y


def matmul_gated_residual(a, w, x, mod3, k_gate, S, tm, tn):
    T, K = a.shape
    N = w.shape[1]
    tpb = S // tm
    nj = N // tn
    return pl.pallas_call(
        _mm_res_kernel,
        grid=(nj, T // tm),
        in_specs=[pl.BlockSpec((tm, K), lambda j, i: (i, 0)),
                  pl.BlockSpec((K, tn), lambda j, i: (0, j)),
                  pl.BlockSpec((tm, tn), lambda j, i: (i, j)),
                  pl.BlockSpec((None, 1, tn), lambda j, i: (i // tpb, 0, k_gate * nj + j))],
        out_specs=pl.BlockSpec((tm, tn), lambda j, i: (i, j)),
        out_shape=jax.ShapeDtypeStruct((T, N), F32),
        compiler_params=_params(("parallel", "parallel")),
        name="out_proj",
    )(a, w, x, mod3)


SUBLANES = 8


def _pack_pair(lo, hi):
    lo_bits = pltpu.bitcast(lo.astype(BF16).astype(F32), jnp.uint32)
    hi_bits = pltpu.bitcast(hi.astype(BF16).astype(F32), jnp.uint32)
    return (lo_bits >> 16) | hi_bits


def _unpack_pair(w):
    lo = pltpu.bitcast(w << 16, F32)
    hi = pltpu.bitcast(w & jnp.uint32(0xFFFF0000), F32)
    return lo, hi


def _store_token_tiles(ref, lead, v):
    rows, d = v.shape
    half = d // 2
    for s in range(SUBLANES):
        w = _pack_pair(v[:, s * LANES:(s + 1) * LANES], v[:, half + s * LANES:half + (s + 1) * LANES])
        ref[lead + (pl.ds(s, rows, stride=SUBLANES), slice(None))] = w


def _load_token_tiles(ref, lead, rows, s):
    return _unpack_pair(ref[lead + (pl.ds(s, rows, stride=SUBLANES), slice(None))])


def _router_kernel(x_ref, g_ref, sc_ref, sh_ref, wh_ref, wl_ref, b_ref, hp_ref, ti_ref, tw_ref):
    h = _rms(x_ref[...], g_ref[...]) * (1.0 + sc_ref[...]) + sh_ref[...]
    _store_token_tiles(hp_ref, (), h)
    h_hi = h.astype(BF16)
    h_lo = (h - h_hi.astype(F32)).astype(BF16)
    wh = wh_ref[...]
    logits = (jnp.dot(h_hi, wh, preferred_element_type=F32)
              + jnp.dot(h_lo, wh, preferred_element_type=F32)
              + jnp.dot(h_hi, wl_ref[...], preferred_element_type=F32)) + b_ref[...]
    lane_i = lax.broadcasted_iota(jnp.int32, logits.shape, 1)
    lane = lane_i.astype(F32)
    vals = []
    idxs = []
    l = logits
    for _ in range(TOP_K):
        m = jnp.max(l, axis=-1, keepdims=True)
        idx = jnp.min(jnp.where(l == m, lane, float(LANES)), axis=-1, keepdims=True)
        vals.append(m)
        idxs.append(idx)
        l = jnp.where(lane == idx, -jnp.inf, l)
    es = [jnp.exp(v - vals[0]) for v in vals]
    tot = es[0] + es[1] + es[2] + es[3]
    ti = jnp.zeros(logits.shape, jnp.int32)
    tw = jnp.zeros(logits.shape, F32)
    for k in range(TOP_K):
        ti = jnp.where(lane_i == k, idxs[k].astype(jnp.int32), ti)
        tw = jnp.where(lane_i == k, es[k] / tot, tw)
    ti_ref[...] = ti
    tw_ref[...] = tw


def ffn_router(x, gain, mod3, k_scale, k_shift, wr_hi, wr_lo, br_pad, S, tm):
    T, D = x.shape
    tpb = S // tm
    return pl.pallas_call(
        _router_kernel,
        grid=(T // tm,),
        in_specs=[pl.BlockSpec((tm, D), lambda i: (i, 0)),
                  pl.BlockSpec((1, D), lambda i: (0, 0)),
                  pl.BlockSpec((None, 1, D), lambda i: (i // tpb, 0, k_scale)),
                  pl.BlockSpec((None, 1, D), lambda i: (i // tpb, 0, k_shift)),
                  pl.BlockSpec((D, LANES), lambda i: (0, 0)),
                  pl.BlockSpec((D, LANES), lambda i: (0, 0)),
                  pl.BlockSpec((1, LANES), lambda i: (0, 0))],
        out_specs=[pl.BlockSpec((tm * SUBLANES, LANES), lambda i: (i, 0)),
                   pl.BlockSpec((tm, LANES), lambda i: (i, 0)),
                   pl.BlockSpec((tm, LANES), lambda i: (i, 0))],
        out_shape=[jax.ShapeDtypeStruct((T * SUBLANES, LANES), jnp.uint32),
                   jax.ShapeDtypeStruct((T, LANES), jnp.int32),
                   jax.ShapeDtypeStruct((T, LANES), F32)],
        compiler_params=_params(("parallel",)),
        name="ffn_router",
    )(x, gain.reshape(1, D), mod3, mod3, wr_hi, wr_lo, br_pad)


def moe_plan(topi, tm):
    T = topi.shape[0]
    E = N_EXPERTS
    n_assign = T * TOP_K
    n_tiles = n_assign // tm + E
    n_slots = n_tiles * tm
    flat_e = topi.reshape(-1)
    onehot = (flat_e[:, None] == jnp.arange(E, dtype=jnp.int32)[None, :]).astype(jnp.int32)
    csum = jnp.cumsum(onehot, axis=0)
    rank = jnp.sum(csum * onehot, axis=1) - 1
    counts = csum[-1]
    tiles_e = (counts + tm - 1) // tm
    cum_tiles = jnp.cumsum(tiles_e)
    tile_start = cum_tiles - tiles_e
    slot = jnp.sum(onehot * tile_start[None, :], axis=1) * tm + rank
    tile_ids = jnp.arange(n_tiles, dtype=jnp.int32)
    tile_e = jnp.minimum(jnp.sum((tile_ids[:, None] >= cum_tiles[None, :]).astype(jnp.int32), axis=1),
                         E - 1).astype(jnp.int32)
    n_used = cum_tiles[-1]
    empty = tiles_e == 0
    n_empty = jnp.sum(empty.astype(jnp.int32))
    fill_tile = jnp.where(empty, n_used + jnp.cumsum(empty.astype(jnp.int32)) - 1, cum_tiles - 1)
    counts_s = jnp.stack([n_used, n_used + n_empty]).astype(jnp.int32)
    return tile_e, counts_s, fill_tile.astype(jnp.int32), (slot * SUBLANES).astype(jnp.int32), n_tiles


def _expert_ffn(h, wgu, bgu, wd, bd):
    F = D_EXPERT
    gu = jnp.dot(h, wgu, preferred_element_type=F32) + bgu
    glu = jnp.minimum(gu[:, :F], SWIGLU_LIMIT)
    lin = jnp.clip(gu[:, F:], -SWIGLU_LIMIT, SWIGLU_LIMIT)
    act = glu * _sigmoid(SWIGLU_ALPHA * glu) * (lin + 1.0)
    return jnp.dot(act.astype(BF16), wd, preferred_element_type=F32) + bd


def _row_tile(first_row):
    if not isinstance(first_row, int):
        first_row = pl.multiple_of(first_row, SUBLANES)
    return pl.ds(first_row, SUBLANES)


def _dispatch_kernel(cnt_ref, fill_ref, slot_ref, hp_ref, hs_hbm, zbuf, rsem, fsem, *, tb, tm, n_tiles):
    i = pl.program_id(0)
    tile_rows = tm * SUBLANES

    def fill(j):
        return pltpu.make_async_copy(
            zbuf, hs_hbm.at[pl.ds(pl.multiple_of(j * tile_rows, tile_rows), tile_rows)], fsem)

    @pl.when(i == 0)
    def _():
        zbuf[...] = jnp.zeros(zbuf.shape, zbuf.dtype)
        for e in range(N_EXPERTS):
            fill(fill_ref[e]).start()

        def start_unused(j, c):
            fill(j).start()
            return c
        lax.fori_loop(cnt_ref[1], n_tiles, start_unused, 0)
        for e in range(N_EXPERTS):
            fill(fill_ref[e]).wait()

        def wait_unused(j, c):
            fill(j).wait()
            return c
        lax.fori_loop(cnt_ref[1], n_tiles, wait_unused, 0)

    for t in range(tb):
        for k in range(TOP_K):
            a = t * TOP_K + k
            pltpu.make_async_copy(hp_ref.at[_row_tile(t * SUBLANES)], hs_hbm.at[_row_tile(slot_ref[0, a])],
                                  rsem).start(priority=a % 2)
    for k in range(TOP_K):
        pltpu.make_async_copy(hp_ref, hs_hbm.at[pl.ds(0, tb * SUBLANES)], rsem).wait()


def moe_dispatch(hp, plan, tb, tm):
    T = hp.shape[0] // SUBLANES
    tile_e, counts_s, fill_tile, slot8, n_tiles = plan
    nb = T // tb
    kern = functools.partial(_dispatch_kernel, tb=tb, tm=tm, n_tiles=n_tiles)
    grid_spec = pltpu.PrefetchScalarGridSpec(
        num_scalar_prefetch=2,
        grid=(nb,),
        in_specs=[pl.BlockSpec((None, 1, tb * TOP_K), lambda i, c, f: (i, 0, 0), memory_space=pltpu.SMEM),
                  pl.BlockSpec((tb * SUBLANES, LANES), lambda i, c, f: (i, 0))],
        out_specs=pl.BlockSpec(memory_space=pltpu.HBM),
        scratch_shapes=[pltpu.VMEM((tm * SUBLANES, LANES), jnp.uint32),
                        pltpu.SemaphoreType.DMA,
                        pltpu.SemaphoreType.DMA])
    return pl.pallas_call(
        kern,
        grid_spec=grid_spec,
        out_shape=jax.ShapeDtypeStruct((n_tiles * tm * SUBLANES, LANES), jnp.uint32),
        compiler_params=_params(("arbitrary",)),
        name="moe_dispatch",
    )(counts_s, fill_tile, slot8.reshape(nb, 1, tb * TOP_K), hp)


def _moe_sparse_kernel(te_ref, cnt_ref, hs_ref, wgu_ref, bgu_ref, wd_ref, bd_ref, ys_ref, h_s, *, tm):
    i = pl.program_id(0)
    half = h_s.shape[-1] // 2

    @pl.when(i < cnt_ref[0])
    def _():
        for s in range(SUBLANES):
            lo, hi = _load_token_tiles(hs_ref, (), tm, s)
            h_s[:, s * LANES:(s + 1) * LANES] = lo.astype(BF16)
            h_s[:, half + s * LANES:half + (s + 1) * LANES] = hi.astype(BF16)
        y = _expert_ffn(h_s[...], wgu_ref[...], bgu_ref[...], wd_ref[...], bd_ref[...])
        _store_token_tiles(ys_ref, (), y)

    @pl.when(i >= cnt_ref[0])
    def _():
        ys_ref[...] = jnp.zeros(ys_ref.shape, ys_ref.dtype)


def moe_sparse(hs, plan, wgu, bgu, wd, bd, tm):
    E, D, F2 = wgu.shape
    F = F2 // 2
    assert D == 2 * SUBLANES * LANES
    tile_e, counts_s, _, _, n_tiles = plan
    blk = (tm * SUBLANES, LANES)
    grid_spec = pltpu.PrefetchScalarGridSpec(
        num_scalar_prefetch=2,
        grid=(n_tiles,),
        in_specs=[pl.BlockSpec(blk, lambda i, te, c: (jnp.minimum(i, c[0] - 1), 0)),
                  pl.BlockSpec((None, D, F2), lambda i, te, c: (te[i], 0, 0)),
                  pl.BlockSpec((None, 1, F2), lambda i, te, c: (te[i], 0, 0)),
                  pl.BlockSpec((None, F, D), lambda i, te, c: (te[i], 0, 0)),
                  pl.BlockSpec((None, 1, D), lambda i, te, c: (te[i], 0, 0))],
        out_specs=pl.BlockSpec(blk, lambda i, te, c: (i, 0)),
        scratch_shapes=[pltpu.VMEM((tm, D), BF16)])
    return pl.pallas_call(
        functools.partial(_moe_sparse_kernel, tm=tm),
        grid_spec=grid_spec,
        out_shape=jax.ShapeDtypeStruct(hs.shape, jnp.uint32),
        compiler_params=_params(("arbitrary",)),
        name="moe_sparse",
    )(tile_e, counts_s, hs, wgu, bgu.reshape(E, 1, F2), wd, bd.reshape(E, 1, D))


def _combine_kernel(slot_cur, slot_nxt, ys_hbm, tw_ref, x_ref, g_ref, o_ref, ybuf, sem, *, tb, nb):
    i = pl.program_id(0)
    cur = i % 2
    half = x_ref.shape[-1] // 2

    def start_row(slot_ref, a, b, priority):
        t, k = divmod(a, TOP_K) if isinstance(a, int) else (a // TOP_K, a % TOP_K)
        pltpu.make_async_copy(ys_hbm.at[_row_tile(slot_ref[0, a])],
                              ybuf.at[b, _row_tile((k * tb + t) * SUBLANES)], sem.at[b]).start(priority=priority)

    @pl.when(i == 0)
    def _():
        def body(a, c):
            start_row(slot_cur, a, 0, 0)
            return c
        lax.fori_loop(0, tb * TOP_K, body, 0, unroll=8)

    @pl.when(i + 1 < nb)
    def _():
        for a in range(tb * TOP_K):
            start_row(slot_nxt, a, 1 - cur, a % 2)

    pltpu.make_async_copy(ys_hbm.at[pl.ds(0, TOP_K * tb * SUBLANES)], ybuf.at[cur], sem.at[cur]).wait()
    tw = tw_ref[...]
    wks = [tw[:, k:k + 1] for k in range(TOP_K)]
    for s in range(SUBLANES):
        acc_lo = None
        acc_hi = None
        for k in range(TOP_K):
            lo, hi = _unpack_pair(ybuf[cur, pl.ds(k * tb * SUBLANES + s, tb, stride=SUBLANES), :])
            acc_lo = wks[k] * lo if acc_lo is None else acc_lo + wks[k] * lo
            acc_hi = wks[k] * hi if acc_hi is None else acc_hi + wks[k] * hi
        c_lo = slice(s * LANES, (s + 1) * LANES)
        c_hi = slice(half + s * LANES, half + (s + 1) * LANES)
        o_ref[:, c_lo] = x_ref[:, c_lo] + g_ref[:, c_lo] * acc_lo
        o_ref[:, c_hi] = x_ref[:, c_hi] + g_ref[:, c_hi] * acc_hi


def moe_combine(ys, slot8, tw, x, mod3, k_gate, S, tb):
    T, D = x.shape
    tpb = S // tb
    nb = T // tb
    slots = slot8.reshape(nb, 1, tb * TOP_K)

    def slot_spec(index_map):
        return pl.BlockSpec((None, 1, tb * TOP_K), index_map, memory_space=pltpu.SMEM)

    return pl.pallas_call(
        functools.partial(_combine_kernel, tb=tb, nb=nb),
        grid=(nb,),
        in_specs=[slot_spec(lambda i: (i, 0, 0)),
                  slot_spec(lambda i: (jnp.minimum(i + 1, nb - 1), 0, 0)),
                  pl.BlockSpec(memory_space=pltpu.HBM),
                  pl.BlockSpec((tb, LANES), lambda i: (i, 0)),
                  pl.BlockSpec((tb, D), lambda i: (i, 0)),
                  pl.BlockSpec((None, 1, D), lambda i: (i // tpb, 0, k_gate))],
        out_specs=pl.BlockSpec((tb, D), lambda i: (i, 0)),
        out_shape=jax.ShapeDtypeStruct((T, D), F32),
        scratch_shapes=[pltpu.VMEM((2, TOP_K * tb * SUBLANES, LANES), jnp.uint32),
                        pltpu.SemaphoreType.DMA((2,))],
        compiler_params=_params(("arbitrary",)),
        name="moe_combine",
    )(slots, slots, ys, tw, x, mod3)


def moe_sparse_layer(hp, ti, tw, wgu, bgu, wd, bd, x, mod3, k_gate, S, tm_moe, tb):
    plan = moe_plan(ti[:, :TOP_K], tm_moe)
    hs = moe_dispatch(hp, plan, tb, tm_moe)
    ys = moe_sparse(hs, plan, wgu, bgu, wd, bd, tm_moe)
    return moe_combine(ys, plan[3], tw, x, mod3, k_gate, S, tb)


def _final_norm_kernel(x_ref, g_ref, o_ref):
    o_ref[...] = _rms(x_ref[...], g_ref[...])


def final_norm(x, gain, tm):
    T, D = x.shape
    return pl.pallas_call(
        _final_norm_kernel,
        grid=(T // tm,),
        in_specs=[pl.BlockSpec((tm, D), lambda i: (i, 0)),
                  pl.BlockSpec((1, D), lambda i: (0, 0))],
        out_specs=pl.BlockSpec((tm, D), lambda i: (i, 0)),
        out_shape=jax.ShapeDtypeStruct((T, D), F32),
        compiler_params=_params(("parallel",)),
        name="final_norm",
    )(x, gain.reshape(1, D))


def _rot_cols(w):
    half = w.shape[-1] // 2
    return jnp.concatenate([-w[..., half:], w[..., :half]], axis=-1)


def _pad_cols(w, width):
    return jnp.pad(w, ((0, 0), (0, width - w.shape[-1])))


def pack_w_in(w_in):
    D = w_in.shape[0]
    sizes = (MLA_Q_LORA, MLA_KV_LORA, MLA_ROPE, RET_HEADS * RET_DK, RET_HEADS * RET_DK,
             RET_HEADS * RET_DV, RET_HEADS * RET_DV, WIN_HEADS * WIN_HEAD_DIM,
             WIN_KV_HEADS * WIN_HEAD_DIM, WIN_KV_HEADS * WIN_HEAD_DIM, N_BRANCH * D_MODEL)
    offs = np.cumsum((0,) + sizes)
    (c_q, c_kv, k_pe, r_q, r_k, r_v, r_g, w_q, w_k, w_v, gate) = [
        w_in[:, offs[i]:offs[i + 1]] for i in range(len(sizes))]
    filler = jnp.zeros((D, COL_GATE - COL_KPER - LANES), w_in.dtype)
    packed = jnp.concatenate(
        [c_q, c_kv, r_q, r_k, r_v, r_g, w_q, w_k, w_v,
         _pad_cols(k_pe, LANES), _pad_cols(_rot_cols(k_pe), LANES), filler, gate], axis=1)
    return packed.astype(BF16)


def pack_w_uq(w_uq):
    H = MLA_HEADS
    w = w_uq.reshape(MLA_Q_LORA, H, MLA_NOPE + MLA_ROPE)
    nope = w[:, :, :MLA_NOPE]
    pe = w[:, :, MLA_NOPE:]
    z = jnp.zeros((MLA_Q_LORA, H, LANES - MLA_ROPE), w.dtype)
    out = jnp.concatenate([nope, pe, z, _rot_cols(pe), z], axis=-1)
    return out.reshape(MLA_Q_LORA, H * 384).astype(BF16)


def pack_w_ukv(w_ukv):
    H = MLA_HEADS
    w = w_ukv.reshape(MLA_KV_LORA, H, 2, MLA_NOPE)
    return w.transpose(0, 2, 1, 3).reshape(MLA_KV_LORA, 2 * H * MLA_NOPE).astype(BF16)


def rope_consts(S):
    inv_freq = 1.0 / (ROPE_THETA ** (jnp.arange(0, MLA_ROPE, 2, dtype=F32) / MLA_ROPE))
    ang = jnp.arange(S, dtype=F32)[:, None] * inv_freq[None, :]
    cos, sin = jnp.cos(ang), jnp.sin(ang)
    z = jnp.zeros((S, LANES - MLA_ROPE), F32)
    cos_pad = jnp.concatenate([cos, cos, z], axis=1)
    sin_pad = jnp.concatenate([sin, sin, z], axis=1)
    cos4 = jnp.concatenate([cos, cos, cos, cos], axis=1)
    sin4 = jnp.concatenate([sin, sin, sin, sin], axis=1)
    return cos_pad, sin_pad, cos4, sin4


def retention_consts():
    k = np.arange(LANES)[:, None]
    j = np.arange(LANES)[None, :]
    jj = j % RET_DK
    half = RET_DK // 2
    rmat = np.where((jj < half) & (k == j + half), -1.0, 0.0) + np.where((jj >= half) & (k == j - half), 1.0, 0.0)
    ea = np.where((k < RET_DK) & ((j == k) | (j == k + RET_DK)), 1.0, 0.0)
    eb = np.where((k >= RET_DK) & ((j == k) | (j == k - RET_DK)), 1.0, 0.0)
    return (jnp.asarray(rmat, BF16), jnp.asarray(ea, BF16), jnp.asarray(eb, BF16))


def _tile(pref, dim):
    return min(pref, dim)


def kernel(x, c, w_ada, b_ada, norm_mix, w_in, b_gate, q_norm, kv_norm, w_uq, w_ukv, ret_decay, sink,
           w_br_mla, w_br_ret, w_br_win, w_out, norm_ffn, w_router, b_router, w_gu, b_gu, w_down,
           b_down, norm_final):
    B, S, D = x.shape
    L = w_ada.shape[0]
    T = B * S
    E = N_EXPERTS
    cos_pad, sin_pad, cos4, sin4 = rope_consts(S)
    rmat, ea, eb = retention_consts()
    mod = ada_mod(c, w_ada, b_ada)
    xt = x.reshape(T, D)
    tm_big = _tile(1024, S)
    for l in range(L):
        mod3 = mod[l].reshape(B, 1, 6 * D)
        proj = norm_mod_matmul(xt, norm_mix[l], mod3, 1, 0, pack_w_in(w_in[l]), S, tm_big, 1536)
        q, k, v = mla_prep(proj, q_norm[l], kv_norm[l], pack_w_uq(w_uq[l]), pack_w_ukv(w_ukv[l]),
                           cos_pad, sin_pad, S, _tile(512, S))
        o_mla = mla_attention(q, k, v, B, S, _tile(512, S), _tile(512, S))
        log_g = -jax.nn.softplus(ret_decay[l].astype(F32))
        o_ret = retention(proj, log_g, cos4, sin4, rmat, ea, eb, B, S, _tile(512, S))
        o_win = window_attention(proj, sink[l], B, S, 128)
        merged = branch_merge(o_mla, o_ret, o_win, proj, b_gate[l].reshape(1, N_BRANCH * D),
                              w_br_mla[l].astype(BF16), w_br_ret[l].astype(BF16),
                              w_br_win[l].astype(BF16), _tile(512, S), 1024)
        xt = matmul_gated_residual(merged, w_out[l].astype(BF16), xt, mod3, 2, S, tm_big, 1024)
        wr = jnp.pad(w_router[l], ((0, 0), (0, LANES - E)))
        wr_hi = wr.astype(BF16)
        wr_lo = (wr - wr_hi.astype(F32)).astype(BF16)
        br_pad = jnp.concatenate([b_router[l].astype(F32), jnp.full((LANES - E,), NEG_INF, F32)]).reshape(1, LANES)
        hp, ti, tw = ffn_router(xt, norm_ffn[l], mod3, 4, 3, wr_hi, wr_lo, br_pad, S, _tile(512, S))
        xt = moe_sparse_layer(hp, ti, tw, w_gu[l].astype(BF16), b_gu[l], w_down[l].astype(BF16),
                              b_down[l], xt, mod3, 5, S, _tile(512, S), _tile(256, S))
    out = final_norm(xt, norm_final, tm_big)
    return out.reshape(B, S, D)
```

```python
import functools

import jax
import jax.numpy as jnp
import numpy as np
from jax import lax
from jax.experimental import pallas as pl
from jax.experimental.pallas import tpu as pltpu

F32 = jnp.float32
BF16 = jnp.bfloat16

D_MODEL = 2048
MLA_HEADS = 8
MLA_Q_LORA = 512
MLA_KV_LORA = 512
MLA_NOPE = 128
MLA_ROPE = 64
MLA_V = 128
RET_HEADS = 8
RET_DK = 64
RET_DV = 128
WIN_HEADS = 16
WIN_KV_HEADS = 4
WIN_HEAD_DIM = 64
WINDOW = 128
N_BRANCH = 3
BRANCH_WIDTH = 1024
N_EXPERTS = 32
TOP_K = 4
D_EXPERT = 768
SWIGLU_LIMIT = 7.0
SWIGLU_ALPHA = 1.702
ROPE_THETA = 10000.0
EPS = 1e-6
NEG_INF = -1e30
LOG2_E = 1.4426950408889634

LANES = 128
VMEM_LIMIT_BYTES = 56 * 1024 * 1024

COL_CQ = 0
COL_CKV = 512
COL_RQ = 1024
COL_RK = 1536
COL_RV = 2048
COL_RG = 3072
COL_WQ = 4096
COL_WK = 5120
COL_WV = 5376
COL_KPE = 5632
COL_KPER = 5760
COL_GATE = 6144
IN_WIDTH_PACKED = COL_GATE + N_BRANCH * D_MODEL


def _params(sem):
    return pltpu.CompilerParams(dimension_semantics=sem, vmem_limit_bytes=VMEM_LIMIT_BYTES)


def _sigmoid(z):
    return 1.0 / (1.0 + jnp.exp(-z))


def _rms(x, gain):
    return x * lax.rsqrt(jnp.mean(x * x, axis=-1, keepdims=True) + EPS) * gain


def _ada_kernel(c_ref, w_ref, b_ref, o_ref):
    c = c_ref[...]
    ca = (c * _sigmoid(c)).astype(BF16)
    o_ref[...] = jnp.dot(ca, w_ref[...].astype(BF16), preferred_element_type=F32) + b_ref[...]


def ada_mod(c, w_ada, b_ada):
    L, D, N = w_ada.shape
    B = c.shape[0]
    tn = 1536
    return pl.pallas_call(
        _ada_kernel,
        grid=(L, N // tn),
        in_specs=[pl.BlockSpec((B, D), lambda l, j: (0, 0)),
                  pl.BlockSpec((None, D, tn), lambda l, j: (l, 0, j)),
                  pl.BlockSpec((None, 1, tn), lambda l, j: (l, 0, j))],
        out_specs=pl.BlockSpec((None, B, tn), lambda l, j: (l, 0, j)),
        out_shape=jax.ShapeDtypeStruct((L, B, N), F32),
        compiler_params=_params(("parallel", "parallel")),
        name="ada_mod",
    )(c, w_ada, b_ada.reshape(L, 1, N))


def _nmm_kernel(x_ref, g_ref, sc_ref, sh_ref, w_ref, o_ref, h_ref):
    @pl.when(pl.program_id(1) == 0)
    def _():
        gain = g_ref[...] * (1.0 + sc_ref[...])
        shift = sh_ref[...]

        def body(r, c):
            rows = pl.ds(pl.multiple_of(r * 64, 64), 64)
            x = x_ref[rows, :]
            y = x * lax.rsqrt(jnp.mean(x * x, axis=-1, keepdims=True) + EPS)
            h_ref[rows, :] = (y * gain + shift).astype(BF16)
            return c

        lax.fori_loop(0, x_ref.shape[0] // 64, body, 0)

    o_ref[...] = jnp.dot(h_ref[...], w_ref[...], preferred_element_type=F32).astype(o_ref.dtype)


def norm_mod_matmul(x, gain, mod3, k_scale, k_shift, w, S, tm, tn):
    T, D = x.shape
    N = w.shape[1]
    tpb = S // tm
    return pl.pallas_call(
        _nmm_kernel,
        grid=(T // tm, N // tn),
        in_specs=[pl.BlockSpec((tm, D), lambda i, j: (i, 0)),
                  pl.BlockSpec((1, D), lambda i, j: (0, 0)),
                  pl.BlockSpec((None, 1, D), lambda i, j: (i // tpb, 0, k_scale)),
                  pl.BlockSpec((None, 1, D), lambda i, j: (i // tpb, 0, k_shift)),
                  pl.BlockSpec((D, tn), lambda i, j: (0, j))],
        out_specs=pl.BlockSpec((tm, tn), lambda i, j: (i, j)),
        out_shape=jax.ShapeDtypeStruct((T, N), BF16),
        scratch_shapes=[pltpu.VMEM((tm, D), BF16)],
        compiler_params=_params(("parallel", "arbitrary")),
        name="in_proj",
    )(x, gain.reshape(1, D), mod3, mod3, w)


def _mla_prep_kernel(cq_ref, ckv_ref, kpe_ref, kper_ref, qn_ref, kvn_ref, wq_ref, wkv_ref,
                     cos_ref, sin_ref, q_ref, k_ref, v_ref):
    H = MLA_HEADS
    scale = (MLA_NOPE + MLA_ROPE) ** -0.5 * LOG2_E
    cos = cos_ref[...]
    sin = sin_ref[...]
    yq = _rms(cq_ref[...].astype(F32), qn_ref[...]).astype(BF16)
    for h in range(H):
        z = jnp.dot(yq, wq_ref[:, h * 384:(h + 1) * 384], preferred_element_type=F32)
        q_ref[:, h * 256:h * 256 + 128] = (z[:, :128] * scale).astype(BF16)
        q_ref[:, h * 256 + 128:(h + 1) * 256] = (
            (z[:, 128:256] * cos + z[:, 256:384] * sin) * scale).astype(BF16)
    ykv = _rms(ckv_ref[...].astype(F32), kvn_ref[...]).astype(BF16)
    kpe = (kpe_ref[...].astype(F32) * cos + kper_ref[...].astype(F32) * sin).astype(BF16)
    zk = jnp.dot(ykv, wkv_ref[:, :H * MLA_NOPE], preferred_element_type=F32)
    for h in range(H):
        k_ref[:, h * 256:h * 256 + 128] = zk[:, h * 128:(h + 1) * 128].astype(BF16)
        k_ref[:, h * 256 + 128:(h + 1) * 256] = kpe
    v_ref[...] = jnp.dot(ykv, wkv_ref[:, H * MLA_NOPE:], preferred_element_type=F32).astype(BF16)


def mla_prep(proj, q_norm, kv_norm, wq_p, wkv_p, cos_pad, sin_pad, S, tm):
    T = proj.shape[0]
    H = MLA_HEADS
    tpb = S // tm
    return pl.pallas_call(
        _mla_prep_kernel,
        grid=(T // tm,),
        in_specs=[pl.BlockSpec((tm, 512), lambda i: (i, COL_CQ // 512)),
                  pl.BlockSpec((tm, 512), lambda i: (i, COL_CKV // 512)),
                  pl.BlockSpec((tm, 128), lambda i: (i, COL_KPE // 128)),
                  pl.BlockSpec((tm, 128), lambda i: (i, COL_KPER // 128)),
                  pl.BlockSpec((1, 512), lambda i: (0, 0)),
                  pl.BlockSpec((1, 512), lambda i: (0, 0)),
                  pl.BlockSpec((512, H * 384), lambda i: (0, 0)),
                  pl.BlockSpec((512, H * 256), lambda i: (0, 0)),
                  pl.BlockSpec((tm, 128), lambda i: (i % tpb, 0)),
                  pl.BlockSpec((tm, 128), lambda i: (i % tpb, 0))],
        out_specs=[pl.BlockSpec((tm, H * 256), lambda i: (i, 0)),
                   pl.BlockSpec((tm, H * 256), lambda i: (i, 0)),
                   pl.BlockSpec((tm, H * 128), lambda i: (i, 0))],
        out_shape=[jax.ShapeDtypeStruct((T, H * 256), BF16),
                   jax.ShapeDtypeStruct((T, H * 256), BF16),
                   jax.ShapeDtypeStruct((T, H * 128), BF16)],
        compiler_params=_params(("parallel",)),
        name="mla_prep",
    )(proj, proj, proj, proj, q_norm.reshape(1, 512), kv_norm.reshape(1, 512), wq_p, wkv_p,
      cos_pad, sin_pad)


def _mla_attn_kernel(q_ref, k_ref, v_ref, o_ref, *, tk):
    S = k_ref.shape[0]
    n_chunks = S // tk
    heads = range(MLA_HEADS_PER_STEP)
    qs = [q_ref[:, hh * 256:(hh + 1) * 256] for hh in heads]

    def scores(hh, c):
        return lax.dot_general(qs[hh], k_ref[c * tk:(c + 1) * tk, hh * 256:(hh + 1) * 256],
                               (((1,), (1,)), ((), ())), preferred_element_type=F32)

    m = [None for _ in heads]
    l = [None for _ in heads]
    acc = [None for _ in heads]
    s_next = [scores(hh, 0) for hh in heads]
    for c in range(n_chunks):
        for hh in heads:
            s = s_next[hh]
            if c + 1 < n_chunks:
                s_next[hh] = scores(hh, c + 1)
            mc = jnp.max(s, axis=-1, keepdims=True)
            m_new = mc if m[hh] is None else jnp.maximum(m[hh], mc)
            p = jnp.exp2(s - m_new)
            pv = jnp.dot(p.astype(BF16), v_ref[c * tk:(c + 1) * tk, hh * MLA_V:(hh + 1) * MLA_V],
                         preferred_element_type=F32)
            ps = jnp.sum(p, axis=-1, keepdims=True)
            if m[hh] is None:
                l[hh], acc[hh] = ps, pv
            else:
                alpha = jnp.exp2(m[hh] - m_new)
                l[hh] = alpha * l[hh] + ps
                acc[hh] = alpha * acc[hh] + pv
            m[hh] = m_new
    for hh in heads:
        o_ref[:, hh * MLA_V:(hh + 1) * MLA_V] = (acc[hh] / l[hh]).astype(BF16)


MLA_HEADS_PER_STEP = 2


def mla_attention(q, k, v, B, S, tq, tk):
    T = q.shape[0]
    H = MLA_HEADS
    G = MLA_HEADS_PER_STEP
    nq = S // tq
    return pl.pallas_call(
        functools.partial(_mla_attn_kernel, tk=tk),
        grid=(B, H // G, nq),
        in_specs=[pl.BlockSpec((tq, G * 256), lambda b, h, i: (b * nq + i, h)),
                  pl.BlockSpec((S, G * 256), lambda b, h, i: (b, h)),
                  pl.BlockSpec((S, G * 128), lambda b, h, i: (b, h))],
        out_specs=pl.BlockSpec((tq, G * 128), lambda b, h, i: (b * nq + i, h)),
        out_shape=jax.ShapeDtypeStruct((T, H * MLA_V), BF16),
        compiler_params=_params(("parallel", "parallel", "arbitrary")),
        name="mla_attn",
    )(q, k, v)


def _ret_kernel(lg_ref, q_ref, k_ref, v_ref, g_ref, cos_ref, sin_ref, rmat_ref, ea_ref, eb_ref,
                o_ref, qd_s, kd_s, kv_s, st_s, *, C, N):
    pair = pl.program_id(1)
    rmat = rmat_ref[...]
    emats = (ea_ref[...], eb_ref[...])
    heads = range(2)

    def chunk(n):
        return pl.ds(pl.multiple_of(n * C, C), C)

    def rope_body(n, carry):
        sl = chunk(n)
        cos = cos_ref[sl, :]
        sin = sin_ref[sl, :]
        qx = q_ref[sl, :]
        kx = k_ref[sl, :]
        qr = qx.astype(F32) * cos + jnp.dot(qx, rmat, preferred_element_type=F32) * sin
        kr = kx.astype(F32) * cos + jnp.dot(kx, rmat, preferred_element_type=F32) * sin
        qb = qr.astype(BF16)
        kb = (kr * (RET_DK ** -0.5)).astype(BF16)
        for hh in heads:
            qd_s[hh, sl, :] = jnp.dot(qb, emats[hh], preferred_element_type=F32).astype(BF16)
            kd_s[hh, sl, :] = jnp.dot(kb, emats[hh], preferred_element_type=F32).astype(BF16)
        return carry

    lax.fori_loop(0, N, rope_body, 0, unroll=2)

    rowi = lax.broadcasted_iota(jnp.int32, (C, LANES), 0).astype(F32)
    left = lax.broadcasted_iota(jnp.int32, (C, LANES), 1) < RET_DK
    dij = (lax.broadcasted_iota(jnp.int32, (C, C), 0)
           - lax.broadcasted_iota(jnp.int32, (C, C), 1)).astype(F32)
    top = lax.broadcasted_iota(jnp.int32, (2 * RET_DK, RET_DV), 0) < RET_DK
    vcols = [slice(hh * RET_DV, (hh + 1) * RET_DV) for hh in heads]
    q_w, k_w, d_intra, dec = [], [], [], []
    for hh in heads:
        lgf = lg_ref[0, pair * 2 + hh]
        lgb = lg_ref[1, pair * 2 + hh]
        q_w.append(jnp.exp(jnp.where(left, lgf * (rowi + 1.0), lgb * (C - rowi))))
        k_w.append(jnp.exp(jnp.where(left, lgf * (C - 1.0 - rowi), lgb * rowi)))
        d_intra.append(jnp.exp(jnp.where(dij >= 0, lgf * dij, -lgb * dij)))
        dec.append(jnp.exp(jnp.where(top, lgf * C, lgb * C)))

    def kv_body(n, carry):
        sl = chunk(n)
        for hh in heads:
            kdw_t = (kd_s[hh, sl, :].astype(F32) * k_w[hh]).T.astype(BF16)
            kv_s[hh, n] = jnp.dot(kdw_t, v_ref[sl, vcols[hh]], preferred_element_type=F32)
        return carry

    lax.fori_loop(0, N, kv_body, 0, unroll=2)

    def fwd_body(n, cf):
        out = []
        for hh in heads:
            st_s[hh, n, 0:RET_DK, :] = cf[hh][0:RET_DK]
            out.append(cf[hh] * dec[hh] + kv_s[hh, n])
        return tuple(out)

    zero = jnp.zeros((2 * RET_DK, RET_DV), F32)
    lax.fori_loop(0, N, fwd_body, (zero, zero))

    def bwd_body(t, cb):
        n = N - 1 - t
        out = []
        for hh in heads:
            st_s[hh, n, RET_DK:2 * RET_DK, :] = cb[hh][RET_DK:2 * RET_DK]
            out.append(cb[hh] * dec[hh] + kv_s[hh, n])
        return tuple(out)

    lax.fori_loop(0, N, bwd_body, (zero, zero))

    def out_body(n, carry):
        sl = chunk(n)
        for hh in heads:
            qd = qd_s[hh, sl, :]
            qs = jnp.where(left, qd, jnp.zeros_like(qd))
            s = lax.dot_general(qs, kd_s[hh, sl, :], (((1,), (1,)), ((), ())),
                                preferred_element_type=F32)
            sd = (s * d_intra[hh]).astype(BF16)
            o = jnp.dot(sd, v_ref[sl, vcols[hh]], preferred_element_type=F32)
            o = o + jnp.dot((qd.astype(F32) * q_w[hh]).astype(BF16), st_s[hh, n].astype(BF16),
                            preferred_element_type=F32)
            mu = jnp.mean(o, axis=-1, keepdims=True)
            dlt = o - mu
            var = jnp.mean(dlt * dlt, axis=-1, keepdims=True)
            on = dlt * lax.rsqrt(var + EPS)
            g = g_ref[sl, vcols[hh]].astype(F32)
            o_ref[sl, vcols[hh]] = (g * _sigmoid(g) * on).astype(BF16)
        return carry

    lax.fori_loop(0, N, out_body, 0, unroll=2)


def retention(proj, log_g, cos4, sin4, rmat, ea, eb, B, S, C):
    T = proj.shape[0]
    N = S // C
    npair = RET_HEADS // 2
    kern = functools.partial(_ret_kernel, C=C, N=N)
    return pl.pallas_call(
        kern,
        grid=(B, npair),
        in_specs=[pl.BlockSpec(memory_space=pltpu.SMEM),
                  pl.BlockSpec((S, 128), lambda b, p: (b, COL_RQ // 128 + p)),
                  pl.BlockSpec((S, 128), lambda b, p: (b, COL_RK // 128 + p)),
                  pl.BlockSpec((S, 256), lambda b, p: (b, COL_RV // 256 + p)),
                  pl.BlockSpec((S, 256), lambda b, p: (b, COL_RG // 256 + p)),
                  pl.BlockSpec((S, 128), lambda b, p: (0, 0)),
                  pl.BlockSpec((S, 128), lambda b, p: (0, 0)),
                  pl.BlockSpec((128, 128), lambda b, p: (0, 0)),
                  pl.BlockSpec((128, 128), lambda b, p: (0, 0)),
                  pl.BlockSpec((128, 128), lambda b, p: (0, 0))],
        out_specs=pl.BlockSpec((S, 256), lambda b, p: (b, p)),
        out_shape=jax.ShapeDtypeStruct((T, RET_HEADS * RET_DV), BF16),
        scratch_shapes=[pltpu.VMEM((2, S, 128), BF16),
                        pltpu.VMEM((2, S, 128), BF16),
                        pltpu.VMEM((2, N, 2 * RET_DK, RET_DV), F32),
                        pltpu.VMEM((2, N, 2 * RET_DK, RET_DV), F32)],
        compiler_params=_params(("parallel", "arbitrary")),
        name="retention",
    )(log_g, proj, proj, proj, proj, cos4, sin4, rmat, ea, eb)


def _win_kernel(sink_ref, q_ref, k_ref, v_ref, o_ref, *, S, tq, span):
    i = pl.program_id(1)
    q0 = i * tq
    kstart = pl.multiple_of(jnp.clip(q0 - WINDOW, 0, S - span), LANES)
    kwin = k_ref[pl.ds(kstart, span), :]
    vwin = v_ref[pl.ds(kstart, span), :]
    qpos = q0 + lax.broadcasted_iota(jnp.int32, (tq, span), 0)
    kpos = kstart + lax.broadcasted_iota(jnp.int32, (tq, span), 1)
    dist_i = jnp.abs(kpos - qpos)
    valid = dist_i <= WINDOW
    dist = dist_i.astype(F32)
    scale = WIN_HEAD_DIM ** -0.5
    R = WIN_HEADS // WIN_KV_HEADS
    d = WIN_HEAD_DIM
    for g in range(WIN_KV_HEADS):
        kg = kwin[:, g * d:(g + 1) * d]
        vg = vwin[:, g * d:(g + 1) * d]
        qg = jnp.concatenate([q_ref[:, (g * R + r) * d:(g * R + r + 1) * d] for r in range(R)], axis=0)
        qg = (qg.astype(F32) * (scale * LOG2_E)).astype(BF16)
        sg = lax.dot_general(qg, kg, (((1,), (1,)), ((), ())), preferred_element_type=F32)
        ps = []
        inv = []
        for r in range(R):
            h = g * R + r
            slope = float(2.0 ** (-8.0 * (h + 1) / WIN_HEADS) * LOG2_E)
            sink = sink_ref[0, h] * LOG2_E
            s = sg[r * tq:(r + 1) * tq] - slope * dist
            s = jnp.where(valid, s, NEG_INF)
            m = jnp.maximum(jnp.max(s, axis=-1, keepdims=True), sink)
            p = jnp.exp2(s - m)
            denom = jnp.sum(p, axis=-1, keepdims=True) + jnp.exp2(sink - m)
            ps.append(p.astype(BF16))
            inv.append(1.0 / denom)
        og = jnp.dot(jnp.concatenate(ps, axis=0), vg, preferred_element_type=F32)
        for r in range(R):
            h = g * R + r
            o_ref[:, h * d:(h + 1) * d] = (og[r * tq:(r + 1) * tq] * inv[r]).astype(BF16)


def window_attention(proj, sink, B, S, tq):
    T = proj.shape[0]
    nq = S // tq
    span = min(tq + 2 * WINDOW, S)
    kern = functools.partial(_win_kernel, S=S, tq=tq, span=span)
    return pl.pallas_call(
        kern,
        grid=(B, nq),
        in_specs=[pl.BlockSpec(memory_space=pltpu.SMEM),
                  pl.BlockSpec((tq, 1024), lambda b, i: (b * nq + i, COL_WQ // 1024)),
                  pl.BlockSpec((S, 256), lambda b, i: (b, COL_WK // 256)),
                  pl.BlockSpec((S, 256), lambda b, i: (b, COL_WV // 256))],
        out_specs=pl.BlockSpec((tq, 1024), lambda b, i: (b * nq + i, 0)),
        out_shape=jax.ShapeDtypeStruct((T, WIN_HEADS * WIN_HEAD_DIM), BF16),
        compiler_params=_params(("parallel", "arbitrary")),
        name="window_attn",
    )(sink.reshape(1, WIN_HEADS), proj, proj, proj)


def _branch_kernel(om_ref, or_ref, ow_ref, g0_ref, g1_ref, g2_ref, b0_ref, b1_ref, b2_ref,
                   wm_ref, wr_ref, ww_ref, o_ref):
    def term(o, g, b, w):
        gate = _sigmoid(g[...].astype(F32) + b[...])
        return gate * jnp.dot(o[...], w[...], preferred_element_type=F32)

    acc = term(om_ref, g0_ref, b0_ref, wm_ref)
    acc = acc + term(or_ref, g1_ref, b1_ref, wr_ref)
    acc = acc + term(ow_ref, g2_ref, b2_ref, ww_ref)
    o_ref[...] = acc.astype(BF16)


def branch_merge(o_mla, o_ret, o_win, proj, b_gate, w_mla, w_ret, w_win, tm, tn):
    T = proj.shape[0]
    D = D_MODEL
    W = BRANCH_WIDTH
    nj = D // tn
    gate0 = COL_GATE // tn
    o_spec = pl.BlockSpec((tm, W), lambda j, i: (i, 0))
    w_spec = pl.BlockSpec((W, tn), lambda j, i: (0, j))

    def g_spec(br):
        return pl.BlockSpec((tm, tn), lambda j, i: (i, gate0 + br * nj + j))

    def b_spec(br):
        return pl.BlockSpec((1, tn), lambda j, i: (0, br * nj + j))

    return pl.pallas_call(
        _branch_kernel,
        grid=(nj, T // tm),
        in_specs=[o_spec, o_spec, o_spec, g_spec(0), g_spec(1), g_spec(2),
                  b_spec(0), b_spec(1), b_spec(2), w_spec, w_spec, w_spec],
        out_specs=pl.BlockSpec((tm, tn), lambda j, i: (i, j)),
        out_shape=jax.ShapeDtypeStruct((T, D), BF16),
        compiler_params=_params(("parallel", "parallel")),
        name="branch_merge",
    )(o_mla, o_ret, o_win, proj, proj, proj, b_gate, b_gate, b_gate, w_mla, w_ret, w_win)


def _mm_res_kernel(a_ref, w_ref, x_ref, g_ref, o_ref):
    y = jnp.dot(a_ref[...], w_ref[...], preferred_element_type=F32)
    o_ref[...] = x_ref[...] + g_ref[...] * y


def matmul_gated_residual(a, w, x, mod3, k_gate, S, tm, tn):
    T, K = a.shape
    N = w.shape[1]
    tpb = S // tm
    nj = N // tn
    return pl.pallas_call(
        _mm_res_kernel,
        grid=(nj, T // tm),
        in_specs=[pl.BlockSpec((tm, K), lambda j, i: (i, 0)),
                  pl.BlockSpec((K, tn), lambda j, i: (0, j)),
                  pl.BlockSpec((tm, tn), lambda j, i: (i, j)),
                  pl.BlockSpec((None, 1, tn), lambda j, i: (i // tpb, 0, k_gate * nj + j))],
        out_specs=pl.BlockSpec((tm, tn), lambda j, i: (i, j)),
        out_shape=jax.ShapeDtypeStruct((T, N), F32),
        compiler_params=_params(("parallel", "parallel")),
        name="out_proj",
    )(a, w, x, mod3)


SUBLANES = 8


def _pack_pair(lo, hi):
    lo_bits = pltpu.bitcast(lo.astype(BF16).astype(F32), jnp.uint32)
    hi_bits = pltpu.bitcast(hi.astype(BF16).astype(F32), jnp.uint32)
    return (lo_bits >> 16) | hi_bits


def _unpack_pair(w):
    lo = pltpu.bitcast(w << 16, F32)
    hi = pltpu.bitcast(w & jnp.uint32(0xFFFF0000), F32)
    return lo, hi


def _store_token_tiles(ref, lead, v):
    rows, d = v.shape
    half = d // 2
    for s in range(SUBLANES):
        w = _pack_pair(v[:, s * LANES:(s + 1) * LANES], v[:, half + s * LANES:half + (s + 1) * LANES])
        ref[lead + (pl.ds(s, rows, stride=SUBLANES), slice(None))] = w


def _load_token_tiles(ref, lead, rows, s):
    return _unpack_pair(ref[lead + (pl.ds(s, rows, stride=SUBLANES), slice(None))])


def _router_kernel(x_ref, g_ref, sc_ref, sh_ref, wh_ref, wl_ref, b_ref, hp_ref, ti_ref, tw_ref):
    h = _rms(x_ref[...], g_ref[...]) * (1.0 + sc_ref[...]) + sh_ref[...]
    _store_token_tiles(hp_ref, (), h)
    h_hi = h.astype(BF16)
    h_lo = (h - h_hi.astype(F32)).astype(BF16)
    wh = wh_ref[...]
    logits = (jnp.dot(h_hi, wh, preferred_element_type=F32)
              + jnp.dot(h_lo, wh, preferred_element_type=F32)
              + jnp.dot(h_hi, wl_ref[...], preferred_element_type=F32)) + b_ref[...]
    lane_i = lax.broadcasted_iota(jnp.int32, logits.shape, 1)
    lane = lane_i.astype(F32)
    vals = []
    idxs = []
    l = logits
    for _ in range(TOP_K):
        m = jnp.max(l, axis=-1, keepdims=True)
        idx = jnp.min(jnp.where(l == m, lane, float(LANES)), axis=-1, keepdims=True)
        vals.append(m)
        idxs.append(idx)
        l = jnp.where(lane == idx, -jnp.inf, l)
    es = [jnp.exp(v - vals[0]) for v in vals]
    tot = es[0] + es[1] + es[2] + es[3]
    ti = jnp.zeros(logits.shape, jnp.int32)
    tw = jnp.zeros(logits.shape, F32)
    for k in range(TOP_K):
        ti = jnp.where(lane_i == k, idxs[k].astype(jnp.int32), ti)
        tw = jnp.where(lane_i == k, es[k] / tot, tw)
    ti_ref[...] = ti
    tw_ref[...] = tw


def ffn_router(x, gain, mod3, k_scale, k_shift, wr_hi, wr_lo, br_pad, S, tm):
    T, D = x.shape
    tpb = S // tm
    return pl.pallas_call(
        _router_kernel,
        grid=(T // tm,),
        in_specs=[pl.BlockSpec((tm, D), lambda i: (i, 0)),
                  pl.BlockSpec((1, D), lambda i: (0, 0)),
                  pl.BlockSpec((None, 1, D), lambda i: (i // tpb, 0, k_scale)),
                  pl.BlockSpec((None, 1, D), lambda i: (i // tpb, 0, k_shift)),
                  pl.BlockSpec((D, LANES), lambda i: (0, 0)),
                  pl.BlockSpec((D, LANES), lambda i: (0, 0)),
                  pl.BlockSpec((1, LANES), lambda i: (0, 0))],
        out_specs=[pl.BlockSpec((tm * SUBLANES, LANES), lambda i: (i, 0)),
                   pl.BlockSpec((tm, LANES), lambda i: (i, 0)),
                   pl.BlockSpec((tm, LANES), lambda i: (i, 0))],
        out_shape=[jax.ShapeDtypeStruct((T * SUBLANES, LANES), jnp.uint32),
                   jax.ShapeDtypeStruct((T, LANES), jnp.int32),
                   jax.ShapeDtypeStruct((T, LANES), F32)],
        compiler_params=_params(("parallel",)),
        name="ffn_router",
    )(x, gain.reshape(1, D), mod3, mod3, wr_hi, wr_lo, br_pad)


def moe_plan(topi, tm):
    T = topi.shape[0]
    E = N_EXPERTS
    n_assign = T * TOP_K
    n_tiles = n_assign // tm + E
    n_slots = n_tiles * tm
    flat_e = topi.reshape(-1)
    onehot = (flat_e[:, None] == jnp.arange(E, dtype=jnp.int32)[None, :]).astype(jnp.int32)
    csum = jnp.cumsum(onehot, axis=0)
    rank = jnp.sum(csum * onehot, axis=1) - 1
    counts = csum[-1]
    tiles_e = (counts + tm - 1) // tm
    cum_tiles = jnp.cumsum(tiles_e)
    tile_start = cum_tiles - tiles_e
    slot = jnp.sum(onehot * tile_start[None, :], axis=1) * tm + rank
    tile_ids = jnp.arange(n_tiles, dtype=jnp.int32)
    tile_e = jnp.minimum(jnp.sum((tile_ids[:, None] >= cum_tiles[None, :]).astype(jnp.int32), axis=1),
                         E - 1).astype(jnp.int32)
    n_used = cum_tiles[-1]
    empty = tiles_e == 0
    n_empty = jnp.sum(empty.astype(jnp.int32))
    fill_tile = jnp.where(empty, n_used + jnp.cumsum(empty.astype(jnp.int32)) - 1, cum_tiles - 1)
    counts_s = jnp.stack([n_used, n_used + n_empty]).astype(jnp.int32)
    return tile_e, counts_s, fill_tile.astype(jnp.int32), (slot * SUBLANES).astype(jnp.int32), n_tiles


def _expert_ffn(h, wgu, bgu, wd, bd):
    F = D_EXPERT
    gu = jnp.dot(h, wgu, preferred_element_type=F32) + bgu
    glu = jnp.minimum(gu[:, :F], SWIGLU_LIMIT)
    lin = jnp.clip(gu[:, F:], -SWIGLU_LIMIT, SWIGLU_LIMIT)
    act = glu * _sigmoid(SWIGLU_ALPHA * glu) * (lin + 1.0)
    return jnp.dot(act.astype(BF16), wd, preferred_element_type=F32) + bd


def _row_tile(first_row):
    if not isinstance(first_row, int):
        first_row = pl.multiple_of(first_row, SUBLANES)
    return pl.ds(first_row, SUBLANES)


def _dispatch_kernel(cnt_ref, fill_ref, slot_ref, hp_ref, hs_hbm, zbuf, rsem, fsem, *, tb, tm, n_tiles):
    i = pl.program_id(0)
    tile_rows = tm * SUBLANES

    def fill(j):
        return pltpu.make_async_copy(
            zbuf, hs_hbm.at[pl.ds(pl.multiple_of(j * tile_rows, tile_rows), tile_rows)], fsem)

    @pl.when(i == 0)
    def _():
        zbuf[...] = jnp.zeros(zbuf.shape, zbuf.dtype)
        for e in range(N_EXPERTS):
            fill(fill_ref[e]).start()

        def start_unused(j, c):
            fill(j).start()
            return c
        lax.fori_loop(cnt_ref[1], n_tiles, start_unused, 0)
        for e in range(N_EXPERTS):
            fill(fill_ref[e]).wait()

        def wait_unused(j, c):
            fill(j).wait()
            return c
        lax.fori_loop(cnt_ref[1], n_tiles, wait_unused, 0)

    for t in range(tb):
        for k in range(TOP_K):
            a = t * TOP_K + k
            pltpu.make_async_copy(hp_ref.at[_row_tile(t * SUBLANES)], hs_hbm.at[_row_tile(slot_ref[0, a])],
                                  rsem).start(priority=a % 2)
    for k in range(TOP_K):
        pltpu.make_async_copy(hp_ref, hs_hbm.at[pl.ds(0, tb * SUBLANES)], rsem).wait()


def moe_dispatch(hp, plan, tb, tm):
    T = hp.shape[0] // SUBLANES
    tile_e, counts_s, fill_tile, slot8, n_tiles = plan
    nb = T // tb
    kern = functools.partial(_dispatch_kernel, tb=tb, tm=tm, n_tiles=n_tiles)
    grid_spec = pltpu.PrefetchScalarGridSpec(
        num_scalar_prefetch=2,
        grid=(nb,),
        in_specs=[pl.BlockSpec((None, 1, tb * TOP_K), lambda i, c, f: (i, 0, 0), memory_space=pltpu.SMEM),
                  pl.BlockSpec((tb * SUBLANES, LANES), lambda i, c, f: (i, 0))],
        out_specs=pl.BlockSpec(memory_space=pltpu.HBM),
        scratch_shapes=[pltpu.VMEM((tm * SUBLANES, LANES), jnp.uint32),
                        pltpu.SemaphoreType.DMA,
                        pltpu.SemaphoreType.DMA])
    return pl.pallas_call(
        kern,
        grid_spec=grid_spec,
        out_shape=jax.ShapeDtypeStruct((n_tiles * tm * SUBLANES, LANES), jnp.uint32),
        compiler_params=_params(("arbitrary",)),
        name="moe_dispatch",
    )(counts_s, fill_tile, slot8.reshape(nb, 1, tb * TOP_K), hp)


def _moe_sparse_kernel(te_ref, cnt_ref, hs_ref, wgu_ref, bgu_ref, wd_ref, bd_ref, ys_ref, h_s, *, tm):
    i = pl.program_id(0)
    half = h_s.shape[-1] // 2

    @pl.when(i < cnt_ref[0])
    def _():
        for s in range(SUBLANES):
            lo, hi = _load_token_tiles(hs_ref, (), tm, s)
            h_s[:, s * LANES:(s + 1) * LANES] = lo.astype(BF16)
            h_s[:, half + s * LANES:half + (s + 1) * LANES] = hi.astype(BF16)
        y = _expert_ffn(h_s[...], wgu_ref[...], bgu_ref[...], wd_ref[...], bd_ref[...])
        _store_token_tiles(ys_ref, (), y)

    @pl.when(i >= cnt_ref[0])
    def _():
        ys_ref[...] = jnp.zeros(ys_ref.shape, ys_ref.dtype)


def moe_sparse(hs, plan, wgu, bgu, wd, bd, tm):
    E, D, F2 = wgu.shape
    F = F2 // 2
    assert D == 2 * SUBLANES * LANES
    tile_e, counts_s, _, _, n_tiles = plan
    blk = (tm * SUBLANES, LANES)
    grid_spec = pltpu.PrefetchScalarGridSpec(
        num_scalar_prefetch=2,
        grid=(n_tiles,),
        in_specs=[pl.BlockSpec(blk, lambda i, te, c: (jnp.minimum(i, c[0] - 1), 0)),
                  pl.BlockSpec((None, D, F2), lambda i, te, c: (te[i], 0, 0)),
                  pl.BlockSpec((None, 1, F2), lambda i, te, c: (te[i], 0, 0)),
                  pl.BlockSpec((None, F, D), lambda i, te, c: (te[i], 0, 0)),
                  pl.BlockSpec((None, 1, D), lambda i, te, c: (te[i], 0, 0))],
        out_specs=pl.BlockSpec(blk, lambda i, te, c: (i, 0)),
        scratch_shapes=[pltpu.VMEM((tm, D), BF16)])
    return pl.pallas_call(
        functools.partial(_moe_sparse_kernel, tm=tm),
        grid_spec=grid_spec,
        out_shape=jax.ShapeDtypeStruct(hs.shape, jnp.uint32),
        compiler_params=_params(("arbitrary",)),
        name="moe_sparse",
    )(tile_e, counts_s, hs, wgu, bgu.reshape(E, 1, F2), wd, bd.reshape(E, 1, D))


def _combine_kernel(slot_cur, slot_nxt, ys_hbm, tw_ref, x_ref, g_ref, o_ref, ybuf, sem, *, tb, nb):
    i = pl.program_id(0)
    cur = i % 2
    half = x_ref.shape[-1] // 2

    def start_row(slot_ref, a, b, priority):
        t, k = divmod(a, TOP_K) if isinstance(a, int) else (a // TOP_K, a % TOP_K)
        pltpu.make_async_copy(ys_hbm.at[_row_tile(slot_ref[0, a])],
                              ybuf.at[b, _row_tile((k * tb + t) * SUBLANES)], sem.at[b]).start(priority=priority)

    @pl.when(i == 0)
    def _():
        def body(a, c):
            start_row(slot_cur, a, 0, 0)
            return c
        lax.fori_loop(0, tb * TOP_K, body, 0, unroll=8)

    @pl.when(i + 1 < nb)
    def _():
        for a in range(tb * TOP_K):
            start_row(slot_nxt, a, 1 - cur, a % 2)

    pltpu.make_async_copy(ys_hbm.at[pl.ds(0, TOP_K * tb * SUBLANES)], ybuf.at[cur], sem.at[cur]).wait()
    tw = tw_ref[...]
    wks = [tw[:, k:k + 1] for k in range(TOP_K)]
    for s in range(SUBLANES):
        acc_lo = None
        acc_hi = None
        for k in range(TOP_K):
            lo, hi = _unpack_pair(ybuf[cur, pl.ds(k * tb * SUBLANES + s, tb, stride=SUBLANES), :])
            acc_lo = wks[k] * lo if acc_lo is None else acc_lo + wks[k] * lo
            acc_hi = wks[k] * hi if acc_hi is None else acc_hi + wks[k] * hi
        c_lo = slice(s * LANES, (s + 1) * LANES)
        c_hi = slice(half + s * LANES, half + (s + 1) * LANES)
        o_ref[:, c_lo] = x_ref[:, c_lo] + g_ref[:, c_lo] * acc_lo
        o_ref[:, c_hi] = x_ref[:, c_hi] + g_ref[:, c_hi] * acc_hi


def moe_combine(ys, slot8, tw, x, mod3, k_gate, S, tb):
    T, D = x.shape
    tpb = S // tb
    nb = T // tb
    slots = slot8.reshape(nb, 1, tb * TOP_K)

    def slot_spec(index_map):
        return pl.BlockSpec((None, 1, tb * TOP_K), index_map, memory_space=pltpu.SMEM)

    return pl.pallas_call(
        functools.partial(_combine_kernel, tb=tb, nb=nb),
        grid=(nb,),
        in_specs=[slot_spec(lambda i: (i, 0, 0)),
                  slot_spec(lambda i: (jnp.minimum(i + 1, nb - 1), 0, 0)),
                  pl.BlockSpec(memory_space=pltpu.HBM),
                  pl.BlockSpec((tb, LANES), lambda i: (i, 0)),
                  pl.BlockSpec((tb, D), lambda i: (i, 0)),
                  pl.BlockSpec((None, 1, D), lambda i: (i // tpb, 0, k_gate))],
        out_specs=pl.BlockSpec((tb, D), lambda i: (i, 0)),
        out_shape=jax.ShapeDtypeStruct((T, D), F32),
        scratch_shapes=[pltpu.VMEM((2, TOP_K * tb * SUBLANES, LANES), jnp.uint32),
                        pltpu.SemaphoreType.DMA((2,))],
        compiler_params=_params(("arbitrary",)),
        name="moe_combine",
    )(slots, slots, ys, tw, x, mod3)


def moe_sparse_layer(hp, ti, tw, wgu, bgu, wd, bd, x, mod3, k_gate, S, tm_moe, tb):
    plan = moe_plan(ti[:, :TOP_K], tm_moe)
    hs = moe_dispatch(hp, plan, tb, tm_moe)
    ys = moe_sparse(hs, plan, wgu, bgu, wd, bd, tm_moe)
    return moe_combine(ys, plan[3], tw, x, mod3, k_gate, S, tb)


def _final_norm_kernel(x_ref, g_ref, o_ref):
    o_ref[...] = _rms(x_ref[...], g_ref[...])


def final_norm(x, gain, tm):
    T, D = x.shape
    return pl.pallas_call(
        _final_norm_kernel,
        grid=(T // tm,),
        in_specs=[pl.BlockSpec((tm, D), lambda i: (i, 0)),
                  pl.BlockSpec((1, D), lambda i: (0, 0))],
        out_specs=pl.BlockSpec((tm, D), lambda i: (i, 0)),
        out_shape=jax.ShapeDtypeStruct((T, D), F32),
        compiler_params=_params(("parallel",)),
        name="final_norm",
    )(x, gain.reshape(1, D))


def _rot_cols(w):
    half = w.shape[-1] // 2
    return jnp.concatenate([-w[..., half:], w[..., :half]], axis=-1)


def _pad_cols(w, width):
    return jnp.pad(w, ((0, 0), (0, width - w.shape[-1])))


def pack_w_in(w_in):
    D = w_in.shape[0]
    sizes = (MLA_Q_LORA, MLA_KV_LORA, MLA_ROPE, RET_HEADS * RET_DK, RET_HEADS * RET_DK,
             RET_HEADS * RET_DV, RET_HEADS * RET_DV, WIN_HEADS * WIN_HEAD_DIM,
             WIN_KV_HEADS * WIN_HEAD_DIM, WIN_KV_HEADS * WIN_HEAD_DIM, N_BRANCH * D_MODEL)
    offs = np.cumsum((0,) + sizes)
    (c_q, c_kv, k_pe, r_q, r_k, r_v, r_g, w_q, w_k, w_v, gate) = [
        w_in[:, offs[i]:offs[i + 1]] for i in range(len(sizes))]
    filler = jnp.zeros((D, COL_GATE - COL_KPER - LANES), w_in.dtype)
    packed = jnp.concatenate(
        [c_q, c_kv, r_q, r_k, r_v, r_g, w_q, w_k, w_v,
         _pad_cols(k_pe, LANES), _pad_cols(_rot_cols(k_pe), LANES), filler, gate], axis=1)
    return packed.astype(BF16)


def pack_w_uq(w_uq):
    H = MLA_HEADS
    w = w_uq.reshape(MLA_Q_LORA, H, MLA_NOPE + MLA_ROPE)
    nope = w[:, :, :MLA_NOPE]
    pe = w[:, :, MLA_NOPE:]
    z = jnp.zeros((MLA_Q_LORA, H, LANES - MLA_ROPE), w.dtype)
    out = jnp.concatenate([nope, pe, z, _rot_cols(pe), z], axis=-1)
    return out.reshape(MLA_Q_LORA, H * 384).astype(BF16)


def pack_w_ukv(w_ukv):
    H = MLA_HEADS
    w = w_ukv.reshape(MLA_KV_LORA, H, 2, MLA_NOPE)
    return w.transpose(0, 2, 1, 3).reshape(MLA_KV_LORA, 2 * H * MLA_NOPE).astype(BF16)


def rope_consts(S):
    inv_freq = 1.0 / (ROPE_THETA ** (jnp.arange(0, MLA_ROPE, 2, dtype=F32) / MLA_ROPE))
    ang = jnp.arange(S, dtype=F32)[:, None] * inv_freq[None, :]
    cos, sin = jnp.cos(ang), jnp.sin(ang)
    z = jnp.zeros((S, LANES - MLA_ROPE), F32)
    cos_pad = jnp.concatenate([cos, cos, z], axis=1)
    sin_pad = jnp.concatenate([sin, sin, z], axis=1)
    cos4 = jnp.concatenate([cos, cos, cos, cos], axis=1)
    sin4 = jnp.concatenate([sin, sin, sin, sin], axis=1)
    return cos_pad, sin_pad, cos4, sin4


def retention_consts():
    k = np.arange(LANES)[:, None]
    j = np.arange(LANES)[None, :]
    jj = j % RET_DK
    half = RET_DK // 2
    rmat = np.where((jj < half) & (k == j + half), -1.0, 0.0) + np.where((jj >= half) & (k == j - half), 1.0, 0.0)
    ea = np.where((k < RET_DK) & ((j == k) | (j == k + RET_DK)), 1.0, 0.0)
    eb = np.where((k >= RET_DK) & ((j == k) | (j == k - RET_DK)), 1.0, 0.0)
    return (jnp.asarray(rmat, BF16), jnp.asarray(ea, BF16), jnp.asarray(eb, BF16))


def _tile(pref, dim):
    return min(pref, dim)


def kernel(x, c, w_ada, b_ada, norm_mix, w_in, b_gate, q_norm, kv_norm, w_uq, w_ukv, ret_decay, sink,
           w_br_mla, w_br_ret, w_br_win, w_out, norm_ffn, w_router, b_router, w_gu, b_gu, w_down,
           b_down, norm_final):
    B, S, D = x.shape
    L = w_ada.shape[0]
    T = B * S
    E = N_EXPERTS
    cos_pad, sin_pad, cos4, sin4 = rope_consts(S)
    rmat, ea, eb = retention_consts()
    mod = ada_mod(c, w_ada, b_ada)
    xt = x.reshape(T, D)
    tm_big = _tile(1024, S)
    for l in range(L):
        mod3 = mod[l].reshape(B, 1, 6 * D)
        proj = norm_mod_matmul(xt, norm_mix[l], mod3, 1, 0, pack_w_in(w_in[l]), S, tm_big, 1536)
        q, k, v = mla_prep(proj, q_norm[l], kv_norm[l], pack_w_uq(w_uq[l]), pack_w_ukv(w_ukv[l]),
                           cos_pad, sin_pad, S, _tile(512, S))
        o_mla = mla_attention(q, k, v, B, S, _tile(512, S), _tile(512, S))
        log_g = -jax.nn.softplus(ret_decay[l].astype(F32))
        o_ret = retention(proj, log_g, cos4, sin4, rmat, ea, eb, B, S, _tile(512, S))
        o_win = window_attention(proj, sink[l], B, S, _tile(256, S))
        merged = branch_merge(o_mla, o_ret, o_win, proj, b_gate[l].reshape(1, N_BRANCH * D),
                              w_br_mla[l].astype(BF16), w_br_ret[l].astype(BF16),
                              w_br_win[l].astype(BF16), _tile(512, S), 1024)
        xt = matmul_gated_residual(merged, w_out[l].astype(BF16), xt, mod3, 2, S, tm_big, 1024)
        wr = jnp.pad(w_router[l], ((0, 0), (0, LANES - E)))
        wr_hi = wr.astype(BF16)
        wr_lo = (wr - wr_hi.astype(F32)).astype(BF16)
        br_pad = jnp.concatenate([b_router[l].astype(F32), jnp.full((LANES - E,), NEG_INF, F32)]).reshape(1, LANES)
        hp, ti, tw = ffn_router(xt, norm_ffn[l], mod3, 4, 3, wr_hi, wr_lo, br_pad, S, _tile(512, S))
        xt = moe_sparse_layer(hp, ti, tw, w_gu[l].astype(BF16), b_gu[l], w_down[l].astype(BF16),
                              b_down[l], xt, mod3, 5, S, _tile(512, S), _tile(256, S))
    out = final_norm(xt, norm_final, tm_big)
    return out.reshape(B, S, D)
```
